```python
import math
import jax, jax.numpy as jnp
from jax import lax
import numpy as np

D_MODEL = 1024
BATCH = 16
SEQ = 2048
DEPTH = 4
DEC_BATCH = 16
DEC_SEQ = 4096
PAST_LEN = 128

N_MIXERS = 3
N_A = (DEPTH + 2) // 3
N_B = (DEPTH + 1) // 3
N_C = DEPTH // 3
W_TOK = D_MODEL
N_MEM = 256
MEM_HEADS = 4
MEM_HEAD_DIM = 64
W_MEM = MEM_HEADS * MEM_HEAD_DIM
W_BRANCH = W_TOK + W_MEM
CONV_WIDTH = 31
RWKV_HEAD = 64
RWKV_HEADS = W_TOK // RWKV_HEAD
RANK_W = 64
RANK_A = 64
P_B = 3 * W_TOK + RANK_W + RANK_A
NA_HEAD_DIM = 64
NA_HEADS = W_TOK // NA_HEAD_DIM
GRID_W = 64
WIN_R = 8
WIN_C = 16
IN_A = 2 * W_TOK + W_MEM + W_BRANCH
IN_B = P_B + W_MEM + W_BRANCH
IN_C = 3 * W_TOK + W_MEM + W_BRANCH
RMS_EPS = 1e-6
LN_EPS = 1e-5
GN_EPS = 64e-5
DECAY_SCALE = math.exp(-0.5)
NEG_INF = -1e30

kernel_name = "hybrid_conv_rwkv7_natten_encoder"


def rms_norm(x, g):
    x32 = x.astype(jnp.float32)
    y = x32 * lax.rsqrt(jnp.mean(x32 * x32, -1, keepdims=True) + RMS_EPS)
    return (y * g.astype(jnp.float32)).astype(x.dtype)


def layer_norm(x, g, b):
    x32 = x.astype(jnp.float32)
    mean = jnp.mean(x32, -1, keepdims=True)
    var = jnp.mean(jnp.square(x32 - mean), -1, keepdims=True)
    y = (x32 - mean) * lax.rsqrt(var + LN_EPS) * g.astype(jnp.float32) + b.astype(jnp.float32)
    return y.astype(x.dtype)


def mixer_conv(h, conv_w, conv_b, ln_g, ln_b):
    z = h[..., :W_TOK] * jax.nn.sigmoid(h[..., W_TOK:])
    z = lax.conv_general_dilated(z, conv_w[:, None, :], (1,), [(CONV_WIDTH // 2, CONV_WIDTH // 2)],
                                 dimension_numbers=('NWC', 'WIO', 'NWC'), feature_group_count=W_TOK) + conv_b
    return jax.nn.silu(layer_norm(z, ln_g, ln_b))


def wkv_scan(r, w, kk, b, v, k, reverse):
    bsz, _, nh, n = r.shape
    xs = tuple(jnp.swapaxes(z, 0, 1) for z in (r, w, kk, b, v, k))

    def step(S, inp):
        r_t, w_t, kk_t, b_t, v_t, k_t = inp
        sa = jnp.einsum('bhvk,bhk->bhv', S, kk_t)
        S = S * w_t[:, :, None, :] - sa[..., None] * b_t[:, :, None, :] + v_t[..., None] * k_t[:, :, None, :]
        return S, jnp.einsum('bhvk,bhk->bhv', S, r_t)

    S0 = jnp.zeros((bsz, nh, n, n), jnp.float32)
    _, ys = lax.scan(step, S0, xs, reverse=reverse)
    return jnp.swapaxes(ys, 0, 1)


def mixer_rwkv(p, mu, w0, w_up, a0, a_up, k_k, k_a, r_k, ln_g, ln_b):
    f32 = jnp.float32
    bsz, t, _ = p.shape
    hs = (bsz, t, RWKV_HEADS, RWKV_HEAD)
    shifted = (jnp.pad(p, ((0, 0), (1, 0), (0, 0)))[:, :t],
               jnp.pad(p, ((0, 0), (0, 1), (0, 0)))[:, 1:])
    kg = k_k.astype(f32).reshape(RWKV_HEADS, RWKV_HEAD)
    ka = k_a.astype(f32).reshape(RWKV_HEADS, RWKV_HEAD)
    rk = r_k.astype(f32)
    wkv = []
    bonus = []
    for d in range(2):
        xd = p + mu[d] * (shifted[d] - p)
        r = xd[..., :W_TOK]
        k = xd[..., W_TOK:2 * W_TOK]
        v = xd[..., 2 * W_TOK:3 * W_TOK]
        wd = xd[..., 3 * W_TOK:3 * W_TOK + RANK_W]
        ad = xd[..., 3 * W_TOK + RANK_W:]
        decay = jnp.exp(-DECAY_SCALE * jax.nn.sigmoid((w0[d] + jnp.tanh(wd) @ w_up[d]).astype(f32)))
        a = jax.nn.sigmoid((a0[d] + ad @ a_up[d]).astype(f32))
        r, k, v, decay, a = (z.astype(f32).reshape(hs) for z in (r, k, v, decay, a))
        kk = k * kg
        kk = kk * lax.rsqrt(jnp.maximum(jnp.sum(kk * kk, -1, keepdims=True), 1e-24))
        k_eff = k * (1.0 + (a - 1.0) * ka)
        wkv.append(wkv_scan(r, decay, kk, kk * a, v, k_eff, d == 1))
        bonus.append(jnp.sum(r * k_eff * rk, -1, keepdims=True) * v)
    y = wkv[0] + wkv[1]
    mean = jnp.mean(y, -1, keepdims=True)
    var = jnp.mean(jnp.square(y - mean), -1, keepdims=True)
    y = (y - mean) * lax.rsqrt(var + GN_EPS)
    y = y.reshape(bsz, t, W_TOK) * ln_g.astype(f32) + ln_b.astype(f32) + (bonus[0] + bonus[1]).reshape(bsz, t, W_TOK)
    return y.astype(p.dtype)


def mixer_natten(qkv, rpb):
    f32 = jnp.float32
    bsz, t, _ = qkv.shape
    rows = t // GRID_W
    kr = min(WIN_R, rows)
    gs = (bsz, rows, GRID_W, NA_HEADS, NA_HEAD_DIM)
    q = (qkv[..., :W_TOK] * NA_HEAD_DIM ** -0.5).reshape(gs)
    k = qkv[..., W_TOK:2 * W_TOK].reshape(gs)
    v = qkv[..., 2 * W_TOK:].reshape(gs)
    cols = np.arange(GRID_W)
    col_start = np.clip(cols - WIN_C // 2, 0, GRID_W - WIN_C)
    col_mask = (cols[None, :] >= col_start[:, None]) & (cols[None, :] < col_start[:, None] + WIN_C)
    col_idx = np.clip(cols[None, :] - cols[:, None], -(WIN_C - 1), WIN_C - 1) + WIN_C - 1
    rpb_cols = rpb[:, :, col_idx]

    def row_block(r):
        s = jnp.clip(r - kr // 2, 0, rows - kr)
        qr = lax.dynamic_index_in_dim(q, r, axis=1, keepdims=False)
        kb = lax.dynamic_slice_in_dim(k, s, kr, axis=1)
        vb = lax.dynamic_slice_in_dim(v, s, kr, axis=1)
        sc = jnp.einsum('bqhd,bikhd->bhqik', qr, kb, preferred_element_type=f32)
        bias = jnp.take(rpb_cols, s - r + jnp.arange(kr) + WIN_R - 1, axis=1)
        sc = sc + jnp.transpose(bias, (0, 2, 1, 3)).astype(f32)[None]
        sc = jnp.where(col_mask[None, None, :, None, :], sc, NEG_INF)
        pr = jax.nn.softmax(sc.reshape(bsz, NA_HEADS, GRID_W, kr * GRID_W), -1).reshape(sc.shape)
        return jnp.einsum('bhqik,bikhd->bqhd', pr.astype(vb.dtype), vb)

    out = lax.map(row_block, jnp.arange(rows))
    return jnp.moveaxis(out, 0, 1).reshape(bsz, t, W_TOK)


def memory_attention(qm, mem, g, w_kv):
    bsz, t, _ = qm.shape
    m = mem.shape[1]
    kv = rms_norm(mem, g) @ w_kv
    mk = kv[..., :W_MEM].reshape(bsz, m, MEM_HEADS, MEM_HEAD_DIM)
    mv = kv[..., W_MEM:].reshape(bsz, m, MEM_HEADS, MEM_HEAD_DIM)
    q = (qm * MEM_HEAD_DIM ** -0.5).reshape(bsz, t, MEM_HEADS, MEM_HEAD_DIM)
    sc = jnp.einsum('bthd,bmhd->bhtm', q, mk, preferred_element_type=jnp.float32)
    pr = jax.nn.softmax(sc, -1).astype(mv.dtype)
    return jnp.einsum('bhtm,bmhd->bthd', pr, mv).reshape(bsz, t, W_MEM)


def trunk(x, mem, pre_norm, post_norm, mem_norm, w_mem_kv,
          a_w_in, a_conv_w, a_conv_b, a_ln_g, a_ln_b, a_w_out,
          b_w_in, b_mu, b_w0, b_w_up, b_a0, b_a_up, b_k_k, b_k_a, b_r_k, b_ln_g, b_ln_b, b_w_out,
          c_w_in, c_rpb, c_w_out):
    for l in range(DEPTH):
        kind, j = l % N_MIXERS, l // N_MIXERS
        hn = rms_norm(x, pre_norm[l])
        if kind == 0:
            h = hn @ a_w_in[j]
            nc = 2 * W_TOK
            tok = mixer_conv(h[..., :nc], a_conv_w[j], a_conv_b[j], a_ln_g[j], a_ln_b[j])
            w_out = a_w_out[j]
        elif kind == 1:
            h = hn @ b_w_in[j]
            nc = P_B
            tok = mixer_rwkv(h[..., :nc], b_mu[j], b_w0[j], b_w_up[j], b_a0[j], b_a_up[j],
                             b_k_k[j], b_k_a[j], b_r_k[j], b_ln_g[j], b_ln_b[j])
            w_out = b_w_out[j]
        else:
            h = hn @ c_w_in[j]
            nc = 3 * W_TOK
            tok = mixer_natten(h[..., :nc], c_rpb[j])
            w_out = c_w_out[j]
        mo = memory_attention(h[..., nc:nc + W_MEM], mem, mem_norm[l], w_mem_kv[l])
        gate = h[..., nc + W_MEM:]
        br = jnp.concatenate([tok.astype(h.dtype), mo], -1) * jax.nn.silu(gate)
        x = x + rms_norm(br @ w_out, post_norm[l])
    return x


def setup_inputs(seed: int = 0) -> dict:
    key = jax.random.key(seed)
    ks = jax.random.split(key, 32)
    f32 = jnp.float32

    def nrm(k, shape, scale):
        return jax.random.normal(k, shape, f32) * scale

    D = D_MODEL
    return {
        "x_prompt": nrm(ks[0], (BATCH, SEQ, D), 1.0),
        "x_sample": nrm(ks[1], (DEC_BATCH, DEC_SEQ, D), 1.0),
        "mem_prompt": nrm(ks[2], (BATCH, N_MEM, D), 1.0),
        "mem_sample": nrm(ks[3], (DEC_BATCH, N_MEM, D), 1.0),
        "pre_norm": 1.0 + nrm(ks[4], (DEPTH, D), 0.02),
        "post_norm": 1.0 + nrm(ks[5], (DEPTH, D), 0.02),
        "mem_norm": 1.0 + nrm(ks[6], (DEPTH, D), 0.02),
        "w_mem_kv": nrm(ks[7], (DEPTH, D, 2 * W_MEM), D ** -0.5),
        "a_w_in": nrm(ks[8], (N_A, D, IN_A), D ** -0.5),
        "a_conv_w": nrm(ks[9], (N_A, CONV_WIDTH, W_TOK), CONV_WIDTH ** -0.5),
        "a_conv_b": nrm(ks[10], (N_A, W_TOK), 0.02),
        "a_ln_g": 1.0 + nrm(ks[11], (N_A, W_TOK), 0.02),
        "a_ln_b": nrm(ks[12], (N_A, W_TOK), 0.02),
        "a_w_out": nrm(ks[13], (N_A, W_BRANCH, D), W_BRANCH ** -0.5),
        "b_w_in": nrm(ks[14], (N_B, D, IN_B), D ** -0.5),
        "b_mu": jax.random.uniform(ks[15], (N_B, 2, P_B), f32),
        "b_w0": nrm(ks[16], (N_B, 2, W_TOK), 0.5),
        "b_w_up": nrm(ks[17], (N_B, 2, RANK_W, W_TOK), 0.5 * RANK_W ** -0.5),
        "b_a0": nrm(ks[18], (N_B, 2, W_TOK), 0.3),
        "b_a_up": nrm(ks[19], (N_B, 2, RANK_A, W_TOK), 0.5 * RANK_A ** -0.5),
        "b_k_k": 0.85 + nrm(ks[20], (N_B, W_TOK), 0.05),
        "b_k_a": 1.0 + nrm(ks[21], (N_B, W_TOK), 0.05),
        "b_r_k": nrm(ks[22], (N_B, RWKV_HEADS, RWKV_HEAD), 0.1),
        "b_ln_g": 1.0 + nrm(ks[23], (N_B, W_TOK), 0.02),
        "b_ln_b": nrm(ks[24], (N_B, W_TOK), 0.02),
        "b_w_out": nrm(ks[25], (N_B, W_BRANCH, D), W_BRANCH ** -0.5),
        "c_w_in": nrm(ks[26], (N_C, D, IN_C), D ** -0.5),
        "c_rpb": nrm(ks[27], (N_C, NA_HEADS, 2 * WIN_R - 1, 2 * WIN_C - 1), 0.1),
        "c_w_out": nrm(ks[28], (N_C, W_BRANCH, D), W_BRANCH ** -0.5),
    }


def reference(x_prompt, x_sample, mem_prompt, mem_sample, pre_norm, post_norm, mem_norm, w_mem_kv,
              a_w_in, a_conv_w, a_conv_b, a_ln_g, a_ln_b, a_w_out,
              b_w_in, b_mu, b_w0, b_w_up, b_a0, b_a_up, b_k_k, b_k_a, b_r_k, b_ln_g, b_ln_b, b_w_out,
              c_w_in, c_rpb, c_w_out):
    weights = (pre_norm, post_norm, mem_norm, w_mem_kv,
               a_w_in, a_conv_w, a_conv_b, a_ln_g, a_ln_b, a_w_out,
               b_w_in, b_mu, b_w0, b_w_up, b_a0, b_a_up, b_k_k, b_k_a, b_r_k, b_ln_g, b_ln_b, b_w_out,
               c_w_in, c_rpb, c_w_out)
    y_prompt = trunk(x_prompt, mem_prompt, *weights)
    y_sample = trunk(x_sample, mem_sample, *weights)
    return (y_prompt, y_sample)
```

```python
import functools
import math

import numpy as np
import jax
import jax.numpy as jnp
from jax import lax
from jax.experimental import pallas as pl
from jax.experimental.pallas import tpu as pltpu

D_MODEL = 1024
DEPTH = 4
W_TOK = D_MODEL
N_MEM = 256
MEM_HEADS = 4
MEM_HEAD_DIM = 64
W_MEM = MEM_HEADS * MEM_HEAD_DIM
W_BRANCH = W_TOK + W_MEM
CONV_WIDTH = 31
CONV_HALO = 16
HEAD_DIM = 64
RANK_W = 64
RANK_A = 64
P_B = 3 * W_TOK + RANK_W + RANK_A
GRID_W = 64
WIN_R = 8
WIN_C = 16
RMS_EPS = 1e-6
LN_EPS = 1e-5
GN_EPS = 64e-5
DECAY_SCALE = math.exp(-0.5)
NEG_INF = -1e30

SLAB = 256
HEADS_PER_SLAB = SLAB // HEAD_DIM
N_SLABS = W_TOK // SLAB
CHUNK = 64
VMEM_LIMIT_BYTES = 56 * 1024 * 1024

F32 = jnp.float32
BF16 = jnp.bfloat16
NT_DIMS = (((1,), (1,)), ((), ()))


def _dot(a, b):
    return jnp.dot(a, b, preferred_element_type=F32)


def _dot_nt(a, b):
    return lax.dot_general(a, b, NT_DIMS, preferred_element_type=F32)


def _sigmoid(x):
    return 1.0 / (1.0 + jnp.exp(-x))


def _head_masks(width):
    lane = lax.broadcasted_iota(jnp.int32, (1, width), 1)
    return [(lane >= h * HEAD_DIM) & (lane < (h + 1) * HEAD_DIM) for h in range(width // HEAD_DIM)]


def _params(*sem):
    return pltpu.CompilerParams(dimension_semantics=sem, vmem_limit_bytes=VMEM_LIMIT_BYTES)


def _mem_kv_kernel(mem_ref, g_ref, wkt_ref, wv_ref, kt_ref, v_ref):
    x = mem_ref[0]
    y = x * lax.rsqrt(jnp.mean(x * x, -1, keepdims=True) + RMS_EPS) * g_ref[0]
    yb = y.astype(BF16)
    kt_ref[0, 0] = _dot_nt(wkt_ref[0], yb).astype(BF16)
    v_ref[0, 0] = _dot(yb, wv_ref[0]).astype(BF16)


def _mem_kv(mem, mem_norm, w_mem_kv):
    bsz, m, d = mem.shape
    wkt = jnp.swapaxes(w_mem_kv[:, :, :W_MEM], 1, 2).astype(BF16)
    wv = w_mem_kv[:, :, W_MEM:].astype(BF16)
    return pl.pallas_call(
        _mem_kv_kernel,
        grid=(DEPTH, bsz),
        in_specs=[
            pl.BlockSpec((1, m, d), lambda l, b: (b, 0, 0)),
            pl.BlockSpec((1, 1, d), lambda l, b: (l, 0, 0)),
            pl.BlockSpec((1, W_MEM, d), lambda l, b: (l, 0, 0)),
            pl.BlockSpec((1, d, W_MEM), lambda l, b: (l, 0, 0)),
        ],
        out_specs=[
            pl.BlockSpec((1, 1, W_MEM, m), lambda l, b: (l, b, 0, 0)),
            pl.BlockSpec((1, 1, m, W_MEM), lambda l, b: (l, b, 0, 0)),
        ],
        out_shape=[
            jax.ShapeDtypeStruct((DEPTH, bsz, W_MEM, m), BF16),
            jax.ShapeDtypeStruct((DEPTH, bsz, m, W_MEM), BF16),
        ],
        compiler_params=_params("arbitrary", "arbitrary"),
        name="mem_kv",
    )(mem, mem_norm.reshape(DEPTH, 1, d), wkt, wv)


def _in_proj_kernel(x_ref, g_ref, w_ref, kt_ref, mv_ref, h_ref, sg_ref, mog_ref, *, nc):
    x = x_ref[0]
    hn = (x * lax.rsqrt(jnp.mean(x * x, -1, keepdims=True) + RMS_EPS) * g_ref[...]).astype(BF16)
    h_ref[0] = _dot(hn, w_ref[:, :nc]).astype(h_ref.dtype)
    q = _dot(hn, w_ref[:, nc:nc + W_MEM]) * (MEM_HEAD_DIM ** -0.5)
    gate = _dot(hn, w_ref[:, nc + W_MEM:])
    sgate = gate * _sigmoid(gate)
    sg_ref[0] = sgate[:, :W_TOK]
    kt = kt_ref[0, 0]
    mv = mv_ref[0, 0]
    mo = jnp.zeros(q.shape, F32)
    for m in _head_masks(W_MEM):
        sc = _dot(jnp.where(m, q, 0.0).astype(BF16), kt)
        e = jnp.exp(sc - jnp.max(sc, -1, keepdims=True))
        pr = e / jnp.sum(e, -1, keepdims=True)
        mo = jnp.where(m, _dot(pr.astype(BF16), mv), mo)
    mog_ref[0] = mo * sgate[:, W_TOK:]


def _in_proj(x, g, w_in, kt, mv, layer, nc, h_dtype, tm):
    bsz, t, d = x.shape
    n_in = w_in.shape[1]
    m = kt.shape[-1]
    return pl.pallas_call(
        functools.partial(_in_proj_kernel, nc=nc),
        grid=(bsz, t // tm),
        in_specs=[
            pl.BlockSpec((1, tm, d), lambda b, j: (b, j, 0)),
            pl.BlockSpec((1, d), lambda b, j: (0, 0)),
            pl.BlockSpec((d, n_in), lambda b, j: (0, 0)),
            pl.BlockSpec((1, 1, W_MEM, m), lambda b, j: (layer, b, 0, 0)),
            pl.BlockSpec((1, 1, m, W_MEM), lambda b, j: (layer, b, 0, 0)),
        ],
        out_specs=[
            pl.BlockSpec((1, tm, nc), lambda b, j: (b, j, 0)),
            pl.BlockSpec((1, tm, W_TOK), lambda b, j: (b, j, 0)),
            pl.BlockSpec((1, tm, W_MEM), lambda b, j: (b, j, 0)),
        ],
        out_shape=[
            jax.ShapeDtypeStruct((bsz, t, nc), h_dtype),
            jax.ShapeDtypeStruct((bsz, t, W_TOK), F32),
            jax.ShapeDtypeStruct((bsz, t, W_MEM), F32),
        ],
        compiler_params=_params("arbitrary", "arbitrary"),
        name="in_proj",
    )(x, g.reshape(1, d), w_in.astype(BF16), kt, mv)


def _out_proj_kernel(tok_ref, sg_ref, mog_ref, x_ref, w_ref, g_ref, o_ref):
    bt = (tok_ref[0].astype(F32) * sg_ref[0]).astype(BF16)
    o = _dot(bt, w_ref[:W_TOK, :]) + _dot(mog_ref[0].astype(BF16), w_ref[W_TOK:, :])
    o = o * lax.rsqrt(jnp.mean(o * o, -1, keepdims=True) + RMS_EPS) * g_ref[...]
    o_ref[0] = x_ref[0] + o


def _out_proj(tok, sg, mog, x, w_out, g, tm):
    bsz, t, d = x.shape
    return pl.pallas_call(
        _out_proj_kernel,
        grid=(bsz, t // tm),
        in_specs=[
            pl.BlockSpec((1, tm, W_TOK), lambda b, j: (b, j, 0)),
            pl.BlockSpec((1, tm, W_TOK), lambda b, j: (b, j, 0)),
            pl.BlockSpec((1, tm, W_MEM), lambda b, j: (b, j, 0)),
            pl.BlockSpec((1, tm, d), lambda b, j: (b, j, 0)),
            pl.BlockSpec((W_BRANCH, d), lambda b, j: (0, 0)),
            pl.BlockSpec((1, d), lambda b, j: (0, 0)),
        ],
        out_specs=pl.BlockSpec((1, tm, d), lambda b, j: (b, j, 0)),
        out_shape=jax.ShapeDtypeStruct((bsz, t, d), F32),
        compiler_params=_params("arbitrary", "arbitrary"),
        name="out_proj",
    )(tok, sg, mog, x, w_out.astype(BF16), g.reshape(1, d))


def _conv_kernel(h_ref, prev_ref, next_ref, cw_ref, cb_ref, lg_ref, lb_ref, o_ref, zp_ref, *, tt):
    j = pl.program_id(1)
    nj = pl.num_programs(1)

    def glu(blk):
        return blk[:, :W_TOK] * _sigmoid(blk[:, W_TOK:])

    zp_ref[0:CONV_HALO, :] = glu(prev_ref[0]) * (j > 0).astype(F32)
    zp_ref[CONV_HALO:CONV_HALO + tt, :] = glu(h_ref[0])
    zp_ref[CONV_HALO + tt:, :] = glu(next_ref[0]) * (j < nj - 1).astype(F32)
    off = CONV_HALO - CONV_WIDTH // 2
    acc = jnp.zeros((tt, W_TOK), F32)
    for tap in range(CONV_WIDTH):
        acc = acc + cw_ref[tap:tap + 1, :] * zp_ref[off + tap:off + tap + tt, :]
    z = acc + cb_ref[...]
    mean = jnp.mean(z, -1, keepdims=True)
    zc = z - mean
    var = jnp.mean(zc * zc, -1, keepdims=True)
    y = zc * lax.rsqrt(var + LN_EPS) * lg_ref[...] + lb_ref[...]
    o_ref[0] = y * _sigmoid(y)


def _mixer_conv(h, conv_w, conv_b, ln_g, ln_b, tt):
    bsz, t, nc = h.shape
    nh = t // CONV_HALO
    per = tt // CONV_HALO
    vec = lambda a: a.reshape(1, W_TOK)
    return pl.pallas_call(
        functools.partial(_conv_kernel, tt=tt),
        grid=(bsz, t // tt),
        in_specs=[
            pl.BlockSpec((1, tt, nc), lambda b, j: (b, j, 0)),
            pl.BlockSpec((1, CONV_HALO, nc), lambda b, j: (b, jnp.maximum(j * per - 1, 0), 0)),
            pl.BlockSpec((1, CONV_HALO, nc), lambda b, j: (b, jnp.minimum((j + 1) * per, nh - 1), 0)),
            pl.BlockSpec((CONV_WIDTH, W_TOK), lambda b, j: (0, 0)),
            pl.BlockSpec((1, W_TOK), lambda b, j: (0, 0)),
            pl.BlockSpec((1, W_TOK), lambda b, j: (0, 0)),
            pl.BlockSpec((1, W_TOK), lambda b, j: (0, 0)),
        ],
        out_specs=pl.BlockSpec((1, tt, W_TOK), lambda b, j: (b, j, 0)),
        out_shape=jax.ShapeDtypeStruct((bsz, t, W_TOK), F32),
        scratch_shapes=[pltpu.VMEM((tt + 2 * CONV_HALO, W_TOK), F32)],
        compiler_params=_params("arbitrary", "arbitrary"),
        name="mixer_conv",
    )(h, h, h, conv_w, vec(conv_b), vec(ln_g), vec(ln_b))


def _natten_kernel(q_ref, k_ref, v_ref, bias_ref, o_ref, *, rows, rq):
    j = pl.program_id(2)
    masks = _head_masks(SLAB)
    kr = WIN_R * GRID_W

    def row(i, carry):
        r = j * rq + i
        s = jnp.clip(r - WIN_R // 2, 0, rows - WIN_R)
        q = q_ref[0, pl.ds(pl.multiple_of(i * GRID_W, GRID_W), GRID_W), :] * (HEAD_DIM ** -0.5)
        qs = jnp.concatenate([jnp.where(m, q, 0).astype(BF16) for m in masks], axis=0)
        start = pl.multiple_of(s * GRID_W, GRID_W)
        sc = _dot_nt(qs, k_ref[0, pl.ds(start, kr), :]) + bias_ref[r - s, 0]
        e = jnp.exp(sc - jnp.max(sc, -1, keepdims=True))
        pr = (e / jnp.sum(e, -1, keepdims=True)).astype(BF16)
        o = _dot(pr, v_ref[0, pl.ds(start, kr), :])
        out = jnp.zeros((GRID_W, SLAB), F32)
        for h, m in enumerate(masks):
            out = jnp.where(m, o[h * GRID_W:(h + 1) * GRID_W, :], out)
        o_ref[0, pl.ds(pl.multiple_of(i * GRID_W, GRID_W), GRID_W), :] = out
        return carry

    lax.fori_loop(0, rq, row, 0)


def _natten_bias(rpb):
    cols = np.arange(GRID_W)
    col_start = np.clip(cols - WIN_C // 2, 0, GRID_W - WIN_C)
    col_mask = (cols[None, :] >= col_start[:, None]) & (cols[None, :] < col_start[:, None] + WIN_C)
    col_idx = np.clip(cols[None, :] - cols[:, None], -(WIN_C - 1), WIN_C - 1) + WIN_C - 1
    rpb_cols = rpb[:, :, col_idx]
    ridx = np.arange(WIN_R)[None, :] + WIN_R - 1 - np.arange(WIN_R)[:, None]
    tab = rpb_cols[:, ridx]
    tab = jnp.where(col_mask[None, None, None], tab, NEG_INF)
    tab = jnp.transpose(tab, (1, 0, 3, 2, 4))
    return tab.reshape(WIN_R, N_SLABS, HEADS_PER_SLAB * GRID_W, WIN_R * GRID_W).astype(F32)


def _mixer_natten(qkv, rpb, rq):
    bsz, t, _ = qkv.shape
    rows = t // GRID_W
    tq = rq * GRID_W
    bias = _natten_bias(rpb)
    return pl.pallas_call(
        functools.partial(_natten_kernel, rows=rows, rq=rq),
        grid=(bsz, N_SLABS, t // tq),
        in_specs=[
            pl.BlockSpec((1, tq, SLAB), lambda b, g, j: (b, j, g)),
            pl.BlockSpec((1, t, SLAB), lambda b, g, j: (b, 0, N_SLABS + g)),
            pl.BlockSpec((1, t, SLAB), lambda b, g, j: (b, 0, 2 * N_SLABS + g)),
            pl.BlockSpec((WIN_R, 1, HEADS_PER_SLAB * GRID_W, WIN_R * GRID_W), lambda b, g, j: (0, g, 0, 0)),
        ],
        out_specs=pl.BlockSpec((1, tq, SLAB), lambda b, g, j: (b, j, g)),
        out_shape=jax.ShapeDtypeStruct((bsz, t, W_TOK), F32),
        compiler_params=_params("arbitrary", "arbitrary", "arbitrary"),
        name="mixer_natten",
    )(qkv, qkv, qkv, bias)


_R_MU_R, _R_MU_K, _R_MU_V, _R_W0, _R_A0, _R_KG, _R_KA, _R_RK, _R_LG, _R_LB = range(10)
_N_PROWS = 16


def _stack_heads(z, masks):
    return jnp.concatenate([jnp.where(m, z, 0.0) for m in masks], axis=0)


def _collapse_heads(zs, c):
    out = zs[0:c]
    for h in range(1, HEADS_PER_SLAB):
        out = out + zs[h * c:(h + 1) * c]
    return out


def _split_dot(lhs_bf16, x):
    hi = x.astype(BF16)
    r1 = x - hi.astype(F32)
    mid = r1.astype(BF16)
    lo = (r1 - mid.astype(F32)).astype(BF16)
    return _dot(lhs_bf16, hi) + _dot(lhs_bf16, mid) + _dot(lhs_bf16, lo)


def _seg_sum(x, bd):
    c = x.shape[0]
    hi = x.astype(BF16)
    lo = (x - hi.astype(F32)).astype(BF16)
    s = _dot(jnp.concatenate([hi, lo], axis=0), bd)
    return s[:c] + s[c:]


def _rwkv_kernel(pr_ref, pk_ref, pv_ref, pwa_ref, prm_ref, muwa_ref, wcat_ref, o_ref, bonus_ref, s_ref, *, t):
    c = CHUNK
    n_chunks = t // c
    masks = _head_masks(SLAB)
    row = lax.broadcasted_iota(jnp.int32, (SLAB, SLAB), 0)
    col = lax.broadcasted_iota(jnp.int32, (SLAB, SLAB), 1)
    same_head = (row // HEAD_DIM) == (col // HEAD_DIM)
    bd_ones = same_head.astype(BF16)
    eye = (row == col).astype(F32)
    crow = lax.broadcasted_iota(jnp.int32, (c, c), 0)
    ccol = lax.broadcasted_iota(jnp.int32, (c, c), 1)
    rid = lax.broadcasted_iota(jnp.int32, (c, 1), 0)
    lane_wa = lax.broadcasted_iota(jnp.int32, (1, RANK_W + RANK_A), 1)

    def prow(d, i):
        return prm_ref[0, d, i:i + 1, :]

    def chunk(ci, d):
        rev = d == 1
        c0 = pl.multiple_of(ci * c, c)
        if rev:
            nidx = jnp.minimum(c0 + c, t - 1)
            valid = (c0 + c < t).astype(F32)
            edge, shift = c - 1, c - 1
            cum_tri = (crow <= ccol).astype(BF16)
            strict, incl = row < col, row <= col
        else:
            nidx = jnp.maximum(c0 - 1, 0)
            valid = (c0 > 0).astype(F32)
            edge, shift = 0, 1
            cum_tri = (crow >= ccol).astype(BF16)
            strict, incl = row > col, row >= col

        def lerp(ref, mu):
            x = ref[0, pl.ds(c0, c), :]
            nb = ref[0, pl.ds(nidx, 1), :] * valid
            sh = jnp.where(rid == edge, nb, pltpu.roll(x, shift, 0))
            return x + mu * (sh - x)

        r = lerp(pr_ref, prow(d, _R_MU_R))
        k = lerp(pk_ref, prow(d, _R_MU_K))
        v = lerp(pv_ref, prow(d, _R_MU_V))
        wa = lerp(pwa_ref, muwa_ref[d:d + 1, :])
        lora = _dot(jnp.where(lane_wa < RANK_W, jnp.tanh(wa), wa).astype(BF16), wcat_ref[0, d])
        lw = -DECAY_SCALE * _sigmoid(prow(d, _R_W0) + lora[:, :SLAB])
        a = _sigmoid(prow(d, _R_A0) + lora[:, SLAB:])
        kk = k * prow(0, _R_KG)
        kk = kk * lax.rsqrt(jnp.maximum(_seg_sum(kk * kk, bd_ones), 1e-24))
        keff = k * (1.0 + (a - 1.0) * prow(0, _R_KA))
        b = kk * a
        bonus = _seg_sum(r * keff * prow(0, _R_RK), bd_ones) * v

        cum = _split_dot(cum_tri, lw)
        tot = cum[0:1] if rev else cum[c - 1:c]
        e_neg = jnp.exp(-cum)
        rt = r * jnp.exp(cum)
        kkt = kk * jnp.exp(cum - lw)
        khat = keff * e_neg
        bhat = b * e_neg
        e_tail = jnp.exp(tot - cum)
        kbar = keff * e_tail
        bbar = b * e_tail

        lhs = jnp.concatenate([_stack_heads(kkt, masks), _stack_heads(rt, masks)], axis=0).astype(BF16)
        rhs = jnp.concatenate([_stack_heads(khat, masks), _stack_heads(bhat, masks)], axis=0).astype(BF16)
        amat = _dot_nt(lhs, rhs)
        akk = jnp.where(strict, amat[:SLAB, :SLAB], 0.0)
        akb = jnp.where(strict, amat[:SLAB, SLAB:], 0.0)
        ark = jnp.where(incl, amat[SLAB:, :SLAB], 0.0)
        arb = jnp.where(incl, amat[SLAB:, SLAB:], 0.0)

        x = eye - akb
        p = akb
        for _ in range(int(math.log2(c)) - 1):
            pb = p.astype(BF16)
            p = _dot(pb, pb)
            x = x + _dot(x.astype(BF16), p.astype(BF16))

        s_bf = s_ref[...].astype(BF16)
        ks = _dot_nt(jnp.concatenate([kkt, rt], axis=0).astype(BF16), s_bf)
        av = _dot(jnp.concatenate([akk, ark], axis=0).astype(BF16), _stack_heads(v, masks).astype(BF16))
        rhs_s = _stack_heads(ks[:c], masks) + av[:SLAB]
        sa_s = _dot(x.astype(BF16), rhs_s.astype(BF16))
        y_s = av[SLAB:] - _dot(arb.astype(BF16), sa_s.astype(BF16))
        y = ks[c:] + _collapse_heads(y_s, c)
        sa = _collapse_heads(sa_s, c)
        vs_t = jnp.concatenate([v, sa], axis=0).T.astype(BF16)
        kb = jnp.concatenate([kbar, -bbar], axis=0).astype(BF16)
        s_ref[...] = s_ref[...] * jnp.exp(tot) + jnp.where(same_head, _dot(vs_t, kb), 0.0)
        return c0, y, bonus

    s_ref[...] = jnp.zeros((SLAB, SLAB), F32)

    def fwd(ci, carry):
        c0, y, bonus = chunk(ci, 0)
        o_ref[0, pl.ds(c0, c), :] = y
        bonus_ref[pl.ds(c0, c), :] = bonus
        return carry

    lax.fori_loop(0, n_chunks, fwd, 0)
    s_ref[...] = jnp.zeros((SLAB, SLAB), F32)

    def bwd(i, carry):
        c0, y, bonus = chunk(n_chunks - 1 - i, 1)
        y = y + o_ref[0, pl.ds(c0, c), :]
        mean = _seg_sum(y, bd_ones) * (1.0 / HEAD_DIM)
        yc = y - mean
        var = _seg_sum(yc * yc, bd_ones) * (1.0 / HEAD_DIM)
        out = yc * lax.rsqrt(var + GN_EPS) * prow(0, _R_LG) + prow(0, _R_LB)
        o_ref[0, pl.ds(c0, c), :] = out + bonus + bonus_ref[pl.ds(c0, c), :]
        return carry

    lax.fori_loop(0, n_chunks, bwd, 0)


def _mixer_rwkv(p, mu, w0, w_up, a0, a_up, k_k, k_a, r_k, ln_g, ln_b):
    bsz, t, _ = p.shape
    slab = lambda a: a.reshape(a.shape[:-1] + (N_SLABS, SLAB))
    per_dir = [slab(mu[:, :W_TOK]), slab(mu[:, W_TOK:2 * W_TOK]), slab(mu[:, 2 * W_TOK:3 * W_TOK]),
               slab(w0), slab(a0)]
    shared = [slab(z.reshape(1, W_TOK)) for z in (k_k, k_a, r_k, ln_g, ln_b)]
    rows = per_dir + [jnp.broadcast_to(z, (2, N_SLABS, SLAB)) for z in shared]
    prm = jnp.stack(rows, axis=2)
    prm = jnp.pad(prm, ((0, 0), (0, 0), (0, _N_PROWS - prm.shape[2]), (0, 0)))
    prm = jnp.transpose(prm, (1, 0, 2, 3)).astype(F32)
    mu_wa = mu[:, 3 * W_TOK:].astype(F32)
    wu = jnp.transpose(slab(w_up), (2, 0, 1, 3))
    au = jnp.transpose(slab(a_up), (2, 0, 1, 3))
    wcat = jnp.concatenate([jnp.concatenate([wu, jnp.zeros_like(wu)], axis=-1),
                            jnp.concatenate([jnp.zeros_like(au), au], axis=-1)], axis=2).astype(BF16)
    n_wa = RANK_W + RANK_A
    return pl.pallas_call(
        functools.partial(_rwkv_kernel, t=t),
        grid=(bsz, N_SLABS),
        in_specs=[
            pl.BlockSpec((1, t, SLAB), lambda b, g: (b, 0, g)),
            pl.BlockSpec((1, t, SLAB), lambda b, g: (b, 0, N_SLABS + g)),
            pl.BlockSpec((1, t, SLAB), lambda b, g: (b, 0, 2 * N_SLABS + g)),
            pl.BlockSpec((1, t, n_wa), lambda b, g: (b, 0, 3 * W_TOK // n_wa)),
            pl.BlockSpec((1, 2, _N_PROWS, SLAB), lambda b, g: (g, 0, 0, 0)),
            pl.BlockSpec((2, n_wa), lambda b, g: (0, 0)),
            pl.BlockSpec((1, 2, n_wa, 2 * SLAB), lambda b, g: (g, 0, 0, 0)),
        ],
        out_specs=pl.BlockSpec((1, t, SLAB), lambda b, g: (b, 0, g)),
        out_shape=jax.ShapeDtypeStruct((bsz, t, W_TOK), F32),
        scratch_shapes=[pltpu.VMEM((t, SLAB), F32), pltpu.VMEM((SLAB, SLAB), F32)],
        compiler_params=_params("arbitrary", "arbitrary"),
        name="mixer_rwkv",
    )(p, p, p, p, prm, mu_wa, wcat)


def _trunk(x, mem, pre_norm, post_norm, mem_norm, w_mem_kv,
           a_w_in, a_conv_w, a_conv_b, a_ln_g, a_ln_b, a_w_out,
           b_w_in, b_mu, b_w0, b_w_up, b_a0, b_a_up, b_k_k, b_k_a, b_r_k, b_ln_g, b_ln_b, b_w_out,
           c_w_in, c_rpb, c_w_out):
    kt, mv = _mem_kv(mem, mem_norm, w_mem_kv)
    t = x.shape[1]
    tm = min(256, t)
    for l in range(DEPTH):
        kind, j = l % 3, l // 3
        if kind == 0:
            nc = 2 * W_TOK
            h, sg, mog = _in_proj(x, pre_norm[l], a_w_in[j], kt, mv, l, nc, F32, tm)
            tok = _mixer_conv(h, a_conv_w[j], a_conv_b[j], a_ln_g[j], a_ln_b[j], min(256, t))
            w_out = a_w_out[j]
        elif kind == 1:
            nc = P_B
            h, sg, mog = _in_proj(x, pre_norm[l], b_w_in[j], kt, mv, l, nc, F32, tm)
            tok = _mixer_rwkv(h, b_mu[j], b_w0[j], b_w_up[j], b_a0[j], b_a_up[j],
                              b_k_k[j], b_k_a[j], b_r_k[j], b_ln_g[j], b_ln_b[j])
            w_out = b_w_out[j]
        else:
            nc = 3 * W_TOK
            h, sg, mog = _in_proj(x, pre_norm[l], c_w_in[j], kt, mv, l, nc, BF16, tm)
            tok = _mixer_natten(h, c_rpb[j], min(WIN_R, t // GRID_W))
            w_out = c_w_out[j]
        x = _out_proj(tok, sg, mog, x, w_out, post_norm[l], tm)
    return x


def kernel(x_prompt, x_sample, mem_prompt, mem_sample, pre_norm, post_norm, mem_norm, w_mem_kv,
           a_w_in, a_conv_w, a_conv_b, a_ln_g, a_ln_b, a_w_out,
           b_w_in, b_mu, b_w0, b_w_up, b_a0, b_a_up, b_k_k, b_k_a, b_r_k, b_ln_g, b_ln_b, b_w_out,
           c_w_in, c_rpb, c_w_out):
    weights = (pre_norm, post_norm, mem_norm, w_mem_kv,
               a_w_in, a_conv_w, a_conv_b, a_ln_g, a_ln_b, a_w_out,
               b_w_in, b_mu, b_w0, b_w_up, b_a0, b_a_up, b_k_k, b_k_a, b_r_k, b_ln_g, b_ln_b, b_w_out,
               c_w_in, c_rpb, c_w_out)
    return (_trunk(x_prompt, mem_prompt, *weights), _trunk(x_sample, mem_sample, *weights))
```

```python
import functools
import math

import numpy as np
import jax
import jax.numpy as jnp
from jax import lax
from jax.experimental import pallas as pl
from jax.experimental.pallas import tpu as pltpu

D_MODEL = 1024
DEPTH = 4
W_TOK = D_MODEL
N_MEM = 256
MEM_HEADS = 4
MEM_HEAD_DIM = 64
W_MEM = MEM_HEADS * MEM_HEAD_DIM
W_BRANCH = W_TOK + W_MEM
CONV_WIDTH = 31
CONV_HALO = 16
HEAD_DIM = 64
RANK_W = 64
RANK_A = 64
P_B = 3 * W_TOK + RANK_W + RANK_A
GRID_W = 64
WIN_R = 8
WIN_C = 16
RMS_EPS = 1e-6
LN_EPS = 1e-5
GN_EPS = 64e-5
DECAY_SCALE = math.exp(-0.5)
NEG_INF = -1e30

SUBLANES = 8
BF16_ROWS = 16
SLAB = 256
HEADS_PER_SLAB = SLAB // HEAD_DIM
N_SLABS = W_TOK // SLAB
CHUNK = 64
TOKEN_TILE = 512
CONV_TILE = 256
NATTEN_ROWS = 4
RWKV_UNROLL = 4
VMEM_LIMIT_BYTES = 56 * 1024 * 1024

F32 = jnp.float32
BF16 = jnp.bfloat16
NT_DIMS = (((1,), (1,)), ((), ()))


def _dot(a, b):
    return jnp.dot(a, b, preferred_element_type=F32)


def _dot_nt(a, b):
    return lax.dot_general(a, b, NT_DIMS, preferred_element_type=F32)


def _sigmoid(x):
    return 1.0 / (1.0 + jnp.exp(-x))


def _head_masks(width):
    lane = lax.broadcasted_iota(jnp.int32, (1, width), 1)
    return [(lane >= h * HEAD_DIM) & (lane < (h + 1) * HEAD_DIM) for h in range(width // HEAD_DIM)]


def _params(*sem):
    return pltpu.CompilerParams(dimension_semantics=sem, vmem_limit_bytes=VMEM_LIMIT_BYTES)


def _mem_kv_kernel(mem_ref, g_ref, wkt_ref, wv_ref, kt_ref, v_ref):
    x = mem_ref[0]
    y = x * lax.rsqrt(jnp.mean(x * x, -1, keepdims=True) + RMS_EPS) * g_ref[0]
    yb = y.astype(BF16)
    kt_ref[0, 0] = _dot_nt(wkt_ref[0], yb).astype(BF16)
    v_ref[0, 0] = _dot(yb, wv_ref[0]).astype(BF16)


def _mem_kv(mem, mem_norm, w_mem_kv):
    bsz, m, d = mem.shape
    wkt = jnp.swapaxes(w_mem_kv[:, :, :W_MEM], 1, 2).astype(BF16)
    wv = w_mem_kv[:, :, W_MEM:].astype(BF16)
    return pl.pallas_call(
        _mem_kv_kernel,
        grid=(DEPTH, bsz),
        in_specs=[
            pl.BlockSpec((1, m, d), lambda l, b: (b, 0, 0)),
            pl.BlockSpec((1, 1, d), lambda l, b: (l, 0, 0)),
            pl.BlockSpec((1, W_MEM, d), lambda l, b: (l, 0, 0)),
            pl.BlockSpec((1, d, W_MEM), lambda l, b: (l, 0, 0)),
        ],
        out_specs=[
            pl.BlockSpec((1, 1, W_MEM, m), lambda l, b: (l, b, 0, 0)),
            pl.BlockSpec((1, 1, m, W_MEM), lambda l, b: (l, b, 0, 0)),
        ],
        out_shape=[
            jax.ShapeDtypeStruct((DEPTH, bsz, W_MEM, m), BF16),
            jax.ShapeDtypeStruct((DEPTH, bsz, m, W_MEM), BF16),
        ],
        compiler_params=_params("arbitrary", "arbitrary"),
        name="mem_kv",
    )(mem, mem_norm.reshape(DEPTH, 1, d), wkt, wv)


def _in_proj_kernel(x_ref, g_ref, w_ref, kt_ref, mv_ref, h_ref, sg_ref, mog_ref, *, nc):
    x = x_ref[0]
    hn = (x * lax.rsqrt(jnp.mean(x * x, -1, keepdims=True) + RMS_EPS) * g_ref[...]).astype(BF16)
    h_ref[0] = _dot(hn, w_ref[:, :nc]).astype(h_ref.dtype)
    q = _dot(hn, w_ref[:, nc:nc + W_MEM]) * (MEM_HEAD_DIM ** -0.5)
    gate = _dot(hn, w_ref[:, nc + W_MEM:])
    sgate = gate * _sigmoid(gate)
    sg_ref[0] = sgate[:, :W_TOK].astype(sg_ref.dtype)
    kt = kt_ref[0, 0]
    mv = mv_ref[0, 0]
    masks = _head_masks(W_MEM)
    sc = [_dot(jnp.where(m, q, 0.0).astype(BF16), kt) for m in masks]
    e = [jnp.exp(s - jnp.max(s, -1, keepdims=True)) for s in sc]
    pr = [(x / jnp.sum(x, -1, keepdims=True)).astype(BF16) for x in e]
    o = [_dot(p, mv) for p in pr]
    mo = jnp.zeros(q.shape, F32)
    for m, x in zip(masks, o):
        mo = jnp.where(m, x, mo)
    mog_ref[0] = (mo * sgate[:, W_TOK:]).astype(mog_ref.dtype)


def _in_proj(x, g, w_in, kt, mv, layer, nc, h_dtype, tm):
    bsz, t, d = x.shape
    n_in = w_in.shape[1]
    m = kt.shape[-1]
    return pl.pallas_call(
        functools.partial(_in_proj_kernel, nc=nc),
        grid=(bsz, t // tm),
        in_specs=[
            pl.BlockSpec((1, tm, d), lambda b, j: (b, j, 0)),
            pl.BlockSpec((1, d), lambda b, j: (0, 0)),
            pl.BlockSpec((d, n_in), lambda b, j: (0, 0)),
            pl.BlockSpec((1, 1, W_MEM, m), lambda b, j: (layer, b, 0, 0)),
            pl.BlockSpec((1, 1, m, W_MEM), lambda b, j: (layer, b, 0, 0)),
        ],
        out_specs=[
            pl.BlockSpec((1, tm, nc), lambda b, j: (b, j, 0)),
            pl.BlockSpec((1, tm, W_TOK), lambda b, j: (b, j, 0)),
            pl.BlockSpec((1, tm, W_MEM), lambda b, j: (b, j, 0)),
        ],
        out_shape=[
            jax.ShapeDtypeStruct((bsz, t, nc), h_dtype),
            jax.ShapeDtypeStruct((bsz, t, W_TOK), BF16),
            jax.ShapeDtypeStruct((bsz, t, W_MEM), BF16),
        ],
        compiler_params=_params("arbitrary", "arbitrary"),
        name="in_proj",
    )(x, g.reshape(1, d), w_in.astype(BF16), kt, mv)


def _gate_project(tok, sg_ref, mog_ref, x_ref, w_ref, g_ref):
    bt = (tok * sg_ref[0]).astype(BF16)
    o = _dot(bt, w_ref[:W_TOK, :]) + _dot(mog_ref[0].astype(BF16), w_ref[W_TOK:, :])
    o = o * lax.rsqrt(jnp.mean(o * o, -1, keepdims=True) + RMS_EPS) * g_ref[...]
    return x_ref[0] + o


def _out_proj_kernel(tok_ref, sg_ref, mog_ref, x_ref, w_ref, g_ref, o_ref):
    o_ref[0] = _gate_project(tok_ref[0].astype(F32), sg_ref, mog_ref, x_ref, w_ref, g_ref)


def _out_proj(tok, sg, mog, x, w_out, g, tm):
    bsz, t, d = x.shape
    return pl.pallas_call(
        _out_proj_kernel,
        grid=(bsz, t // tm),
        in_specs=[
            pl.BlockSpec((1, tm, W_TOK), lambda b, j: (b, j, 0)),
            pl.BlockSpec((1, tm, W_TOK), lambda b, j: (b, j, 0)),
            pl.BlockSpec((1, tm, W_MEM), lambda b, j: (b, j, 0)),
            pl.BlockSpec((1, tm, d), lambda b, j: (b, j, 0)),
            pl.BlockSpec((W_BRANCH, d), lambda b, j: (0, 0)),
            pl.BlockSpec((1, d), lambda b, j: (0, 0)),
        ],
        out_specs=pl.BlockSpec((1, tm, d), lambda b, j: (b, j, 0)),
        out_shape=jax.ShapeDtypeStruct((bsz, t, d), F32),
        compiler_params=_params("arbitrary", "arbitrary"),
        name="out_proj",
    )(tok, sg, mog, x, w_out.astype(BF16), g.reshape(1, d))


def _conv_kernel(h_ref, prev_ref, next_ref, cw_ref, cb_ref, lg_ref, lb_ref, sg_ref, mog_ref, x_ref, w_ref, g_ref,
                 o_ref, zp_ref, q_ref, *, tt):
    j = pl.program_id(1)
    nj = pl.num_programs(1)

    def glu(blk):
        blk = blk.astype(F32)
        return blk[:, :W_TOK] * _sigmoid(blk[:, W_TOK:])

    zp_ref[0:CONV_HALO, :] = glu(prev_ref[0]) * (j > 0).astype(F32)
    zp_ref[CONV_HALO:CONV_HALO + tt, :] = glu(h_ref[0])
    zp_ref[CONV_HALO + tt:, :] = glu(next_ref[0]) * (j < nj - 1).astype(F32)
    off = CONV_HALO - CONV_WIDTH // 2
    n = tt + SUBLANES
    acc = None
    for s in range(SUBLANES):
        q = None
        for a in range((CONV_WIDTH + off - 1) // SUBLANES + 1):
            tap = SUBLANES * a + s - off
            if 0 <= tap < CONV_WIDTH:
                term = cw_ref[tap:tap + 1, :] * zp_ref[SUBLANES * a:SUBLANES * a + n, :]
                q = term if q is None else q + term
        q_ref[...] = q
        part = q_ref[s:s + tt, :]
        acc = part if acc is None else acc + part
    z = acc + cb_ref[...]
    mean = jnp.mean(z, -1, keepdims=True)
    zc = z - mean
    var = jnp.mean(zc * zc, -1, keepdims=True)
    y = zc * lax.rsqrt(var + LN_EPS) * lg_ref[...] + lb_ref[...]
    o_ref[0] = _gate_project(y * _sigmoid(y), sg_ref, mog_ref, x_ref, w_ref, g_ref)


def _mixer_conv_out(h, conv_w, conv_b, ln_g, ln_b, sg, mog, x, w_out, g, tt):
    bsz, t, nc = h.shape
    d = x.shape[-1]
    nh = t // CONV_HALO
    per = tt // CONV_HALO
    vec = lambda a: a.reshape(1, -1)
    tile = lambda width: pl.BlockSpec((1, tt, width), lambda b, j: (b, j, 0))
    whole = lambda shape: pl.BlockSpec(shape, lambda b, j: (0,) * len(shape))
    return pl.pallas_call(
        functools.partial(_conv_kernel, tt=tt),
        grid=(bsz, t // tt),
        in_specs=[
            tile(nc),
            pl.BlockSpec((1, CONV_HALO, nc), lambda b, j: (b, jnp.maximum(j * per - 1, 0), 0)),
            pl.BlockSpec((1, CONV_HALO, nc), lambda b, j: (b, jnp.minimum((j + 1) * per, nh - 1), 0)),
            whole((CONV_WIDTH, W_TOK)), whole((1, W_TOK)), whole((1, W_TOK)), whole((1, W_TOK)),
            tile(W_TOK), tile(W_MEM), tile(d), whole((W_BRANCH, d)), whole((1, d)),
        ],
        out_specs=tile(d),
        out_shape=jax.ShapeDtypeStruct((bsz, t, d), F32),
        scratch_shapes=[pltpu.VMEM((tt + 2 * CONV_HALO, W_TOK), F32), pltpu.VMEM((tt + SUBLANES, W_TOK), F32)],
        compiler_params=_params("arbitrary", "arbitrary"),
        name="mixer_conv_out",
    )(h, h, h, conv_w, vec(conv_b), vec(ln_g), vec(ln_b), sg, mog, x, w_out.astype(BF16), vec(g))


def _natten_kernel(q_ref, k_ref, v_ref, bias_ref, o_ref, *, rows, rq):
    j = pl.program_id(2)
    masks = _head_masks(SLAB)
    kr = WIN_R * GRID_W

    def group(idx):
        r = [j * rq + i for i in idx]
        s = [jnp.clip(x - WIN_R // 2, 0, rows - WIN_R) for x in r]
        start = [pl.multiple_of(x * GRID_W, GRID_W) for x in s]
        q = [q_ref[0, i * GRID_W:(i + 1) * GRID_W, :] * (HEAD_DIM ** -0.5) for i in idx]
        qs = [jnp.concatenate([jnp.where(m, x, 0).astype(BF16) for m in masks], axis=0) for x in q]
        sc = [_dot_nt(x, k_ref[0, pl.ds(st, kr), :]) + bias_ref[ri - si, 0]
              for x, st, ri, si in zip(qs, start, r, s)]
        e = [jnp.exp(x - jnp.max(x, -1, keepdims=True)) for x in sc]
        pr = [(x / jnp.sum(x, -1, keepdims=True)).astype(BF16) for x in e]
        o = [_dot(x, v_ref[0, pl.ds(st, kr), :]) for x, st in zip(pr, start)]
        for i, x in zip(idx, o):
            out = jnp.zeros((GRID_W, SLAB), F32)
            for h, m in enumerate(masks):
                out = jnp.where(m, x[h * GRID_W:(h + 1) * GRID_W, :], out)
            o_ref[0, i * GRID_W:(i + 1) * GRID_W, :] = out.astype(o_ref.dtype)

    for g0 in range(0, rq, NATTEN_ROWS):
        group(list(range(g0, min(g0 + NATTEN_ROWS, rq))))


def _natten_bias(rpb):
    cols = np.arange(GRID_W)
    col_start = np.clip(cols - WIN_C // 2, 0, GRID_W - WIN_C)
    col_mask = (cols[None, :] >= col_start[:, None]) & (cols[None, :] < col_start[:, None] + WIN_C)
    col_idx = np.clip(cols[None, :] - cols[:, None], -(WIN_C - 1), WIN_C - 1) + WIN_C - 1
    rpb_cols = rpb[:, :, col_idx]
    ridx = np.arange(WIN_R)[None, :] + WIN_R - 1 - np.arange(WIN_R)[:, None]
    tab = rpb_cols[:, ridx]
    tab = jnp.where(col_mask[None, None, None], tab, NEG_INF)
    tab = jnp.transpose(tab, (1, 0, 3, 2, 4))
    return tab.reshape(WIN_R, N_SLABS, HEADS_PER_SLAB * GRID_W, WIN_R * GRID_W).astype(F32)


def _mixer_natten(qkv, rpb, rq):
    bsz, t, _ = qkv.shape
    rows = t // GRID_W
    tq = rq * GRID_W
    bias = _natten_bias(rpb)
    return pl.pallas_call(
        functools.partial(_natten_kernel, rows=rows, rq=rq),
        grid=(bsz, N_SLABS, t // tq),
        in_specs=[
            pl.BlockSpec((1, tq, SLAB), lambda b, g, j: (b, j, g)),
            pl.BlockSpec((1, t, SLAB), lambda b, g, j: (b, 0, N_SLABS + g)),
            pl.BlockSpec((1, t, SLAB), lambda b, g, j: (b, 0, 2 * N_SLABS + g)),
            pl.BlockSpec((WIN_R, 1, HEADS_PER_SLAB * GRID_W, WIN_R * GRID_W), lambda b, g, j: (0, g, 0, 0)),
        ],
        out_specs=pl.BlockSpec((1, tq, SLAB), lambda b, g, j: (b, j, g)),
        out_shape=jax.ShapeDtypeStruct((bsz, t, W_TOK), BF16),
        compiler_params=_params("arbitrary", "arbitrary", "arbitrary"),
        name="mixer_natten",
    )(qkv, qkv, qkv, bias)


_R_MU_R, _R_MU_K, _R_MU_V, _R_W0, _R_A0, _R_KG, _R_KA, _R_RK, _R_LG, _R_LB = range(10)
_N_PROWS = 16


def _stack_heads(z, masks):
    return jnp.concatenate([jnp.where(m, z, 0.0) for m in masks], axis=0)


def _collapse_heads(zs, c):
    out = zs[0:c]
    for h in range(1, HEADS_PER_SLAB):
        out = out + zs[h * c:(h + 1) * c]
    return out


def _split_dot(lhs_bf16, x):
    hi = x.astype(BF16)
    r1 = x - hi.astype(F32)
    mid = r1.astype(BF16)
    lo = (r1 - mid.astype(F32)).astype(BF16)
    return _dot(lhs_bf16, hi) + _dot(lhs_bf16, mid) + _dot(lhs_bf16, lo)


def _seg_sum(x, bd):
    c = x.shape[0]
    hi = x.astype(BF16)
    lo = (x - hi.astype(F32)).astype(BF16)
    s = _dot(jnp.concatenate([hi, lo], axis=0), bd)
    return s[:c] + s[c:]


def _rwkv_kernel(pr_ref, pk_ref, pv_ref, pwa_ref, prm_ref, muwa_ref, wcat_ref, o_ref,
                 y_ref, bonus_ref, s_ref, *, t):
    c = CHUNK
    n_chunks = t // c
    masks = _head_masks(SLAB)
    row = lax.broadcasted_iota(jnp.int32, (SLAB, SLAB), 0)
    col = lax.broadcasted_iota(jnp.int32, (SLAB, SLAB), 1)
    same_head = (row // HEAD_DIM) == (col // HEAD_DIM)
    bd_ones = same_head.astype(BF16)
    eye = (row == col).astype(F32)
    crow = lax.broadcasted_iota(jnp.int32, (c, c), 0)
    ccol = lax.broadcasted_iota(jnp.int32, (c, c), 1)
    rid = lax.broadcasted_iota(jnp.int32, (c, 1), 0)
    lane_wa = lax.broadcasted_iota(jnp.int32, (1, RANK_W + RANK_A), 1)

    cum_tri = [(crow >= ccol).astype(BF16), (crow <= ccol).astype(BF16)]
    strict = [row > col, row < col]
    incl = [row >= col, row <= col]
    edge = [0, c - 1]
    shift = [1, c - 1]

    def prow(d, i):
        return prm_ref[0, d, i:i + 1, :]

    def stage_a(lanes):
        ds = [d for _, d in lanes]
        each = lambda f, *cols: [f(*args) for args in zip(*cols)]
        c0s = [pl.multiple_of(ci * c, c) for ci, _ in lanes]
        nidx = [pl.multiple_of(jnp.minimum(c0 + c, t - BF16_ROWS) if d else jnp.maximum(c0 - BF16_ROWS, 0),
                               BF16_ROWS) for c0, d in zip(c0s, ds)]
        valid = [((c0 + c < t) if d else (c0 > 0)).astype(F32) for c0, d in zip(c0s, ds)]

        def lerp(ref, mus):
            def one(c0, ni, va, d, mu):
                x = ref[0, pl.ds(c0, c), :].astype(F32)
                blk = ref[0, pl.ds(ni, BF16_ROWS), :].astype(F32)
                nb = (blk[0:1] if d else blk[BF16_ROWS - 1:BF16_ROWS]) * va
                sh = jnp.where(rid == edge[d], nb, pltpu.roll(x, shift[d], 0))
                return x + mu * (sh - x)
            return each(one, c0s, nidx, valid, ds, mus)

        r = lerp(pr_ref, [prow(d, _R_MU_R) for d in ds])
        k = lerp(pk_ref, [prow(d, _R_MU_K) for d in ds])
        v = lerp(pv_ref, [prow(d, _R_MU_V) for d in ds])
        wa = lerp(pwa_ref, [muwa_ref[d:d + 1, :] for d in ds])
        lora = each(lambda x, d: _dot(jnp.where(lane_wa < RANK_W, jnp.tanh(x), x).astype(BF16), wcat_ref[0, d]),
                    wa, ds)
        lw = each(lambda lo, d: -DECAY_SCALE * _sigmoid(prow(d, _R_W0) + lo[:, :SLAB]), lora, ds)
        a = each(lambda lo, d: _sigmoid(prow(d, _R_A0) + lo[:, SLAB:]), lora, ds)
        kk = each(lambda x: x * prow(0, _R_KG), k)
        ss = each(lambda x: _seg_sum(x * x, bd_ones), kk)
        kk = each(lambda x, s: x * lax.rsqrt(jnp.maximum(s, 1e-24)), kk, ss)
        keff = each(lambda x, aa: x * (1.0 + (aa - 1.0) * prow(0, _R_KA)), k, a)
        b = each(lambda x, aa: x * aa, kk, a)
        rk = each(lambda x, y: _seg_sum(x * y * prow(0, _R_RK), bd_ones), r, keff)
        bonus = each(lambda x, y: x * y, rk, v)

        cum = each(lambda x, d: _split_dot(cum_tri[d], x), lw, ds)
        tot = each(lambda x, d: x[0:1] if d else x[c - 1:c], cum, ds)
        e_neg = each(lambda x: jnp.exp(-x), cum)
        e_tail = each(lambda x, y: jnp.exp(x - y), tot, cum)
        rt = each(lambda x, y: x * jnp.exp(y), r, cum)
        kkt = each(lambda x, y, z: x * jnp.exp(y - z), kk, cum, lw)
        khat = each(lambda x, y: x * y, keff, e_neg)
        bhat = each(lambda x, y: x * y, b, e_neg)
        kb = each(lambda x, y, e: jnp.concatenate([x * e, -(y * e)], axis=0).astype(BF16), keff, b, e_tail)

        lhs = each(lambda x, y: jnp.concatenate([_stack_heads(x, masks), _stack_heads(y, masks)],
                                                axis=0).astype(BF16), kkt, rt)
        rhs = each(lambda x, y: jnp.concatenate([_stack_heads(x, masks), _stack_heads(y, masks)],
                                                axis=0).astype(BF16), khat, bhat)
        amat = each(_dot_nt, lhs, rhs)
        akb = each(lambda m, d: jnp.where(strict[d], m[:SLAB, SLAB:], 0.0), amat, ds)
        a_v = each(lambda m, d: jnp.concatenate([jnp.where(strict[d], m[:SLAB, :SLAB], 0.0),
                                                 jnp.where(incl[d], m[SLAB:, :SLAB], 0.0)], axis=0).astype(BF16),
                   amat, ds)
        arb = each(lambda m, d: jnp.where(incl[d], m[SLAB:, SLAB:], 0.0).astype(BF16), amat, ds)
        av = each(lambda m, x: _dot(m, _stack_heads(x, masks).astype(BF16)), a_v, v)

        x = each(lambda m: eye - m, akb)
        p = each(lambda m: m.astype(BF16), akb)
        for _ in range(int(math.log2(c)) - 1):
            p32 = each(lambda m: _dot(m, m), p)
            p = each(lambda m: m.astype(BF16), p32)
            x = each(lambda xx, pp: xx + _dot(xx.astype(BF16), pp), x, p)
        xinv = each(lambda m: m.astype(BF16), x)
        kr = each(lambda x, y: jnp.concatenate([x, y], axis=0).astype(BF16), kkt, rt)
        return [dict(c0=c0s[i], d=ds[i], kr=kr[i], av=av[i], xinv=xinv[i], arb=arb[i], kb=kb[i], v=v[i],
                     decay=jnp.exp(tot[i]), bonus=bonus[i]) for i in range(len(lanes))]

    def stage_b(la):
        each = lambda f, *cols: [f(*args) for args in zip(*cols)]
        ds = [l["d"] for l in la]
        ks = each(lambda l, d: _dot_nt(l["kr"], s_ref[d].astype(BF16)), la, ds)
        rhs_s = each(lambda l, x: (_stack_heads(x[:c], masks) + l["av"][:SLAB]).astype(BF16), la, ks)
        sa_s = each(lambda l, x: _dot(l["xinv"], x), la, rhs_s)
        y_s = each(lambda l, x: l["av"][SLAB:] - _dot(l["arb"], x.astype(BF16)), la, sa_s)
        y = each(lambda x, z: x[c:] + _collapse_heads(z, c), ks, y_s)
        vs_t = each(lambda l, x: jnp.concatenate([l["v"], _collapse_heads(x, c)], axis=0).T.astype(BF16),
                    la, sa_s)
        upd = each(lambda l, x: jnp.where(same_head, _dot(x, l["kb"]), 0.0), la, vs_t)
        for l, d, u in zip(la, ds, upd):
            s_ref[d] = s_ref[d] * l["decay"] + u
        return y

    s_ref[...] = jnp.zeros((2, SLAB, SLAB), F32)
    half = n_chunks // 2
    u = RWKV_UNROLL

    def lanes_of(i):
        return [(i * u + q, 0) for q in range(u)] + [(n_chunks - 1 - i * u - q, 1) for q in range(u)]

    def run(i):
        la = stage_a(lanes_of(i))
        out = []
        for q in range(u):
            pair = [la[q], la[u + q]]
            out += list(zip(pair, stage_b(pair)))
        return out

    def first(i, carry):
        for l, y in run(i):
            y_ref[pl.ds(l["c0"], c), :] = y
            bonus_ref[pl.ds(l["c0"], c), :] = l["bonus"]
        return carry

    def second(i, carry):
        res = run(i)
        ys = [y + y_ref[pl.ds(l["c0"], c), :] for l, y in res]
        mean = [_seg_sum(y, bd_ones) * (1.0 / HEAD_DIM) for y in ys]
        yc = [y - m for y, m in zip(ys, mean)]
        var = [_seg_sum(z * z, bd_ones) * (1.0 / HEAD_DIM) for z in yc]
        for (l, _), z, vv in zip(res, yc, var):
            out = z * lax.rsqrt(vv + GN_EPS) * prow(0, _R_LG) + prow(0, _R_LB)
            o_ref[0, pl.ds(l["c0"], c), :] = (out + l["bonus"] + bonus_ref[pl.ds(l["c0"], c), :]).astype(o_ref.dtype)
        return carry

    lax.fori_loop(0, half // u, first, 0)
    lax.fori_loop(half // u, n_chunks // u, second, 0)


def _mixer_rwkv(p, mu, w0, w_up, a0, a_up, k_k, k_a, r_k, ln_g, ln_b):
    bsz, t, _ = p.shape
    assert t % (2 * CHUNK * RWKV_UNROLL) == 0
    slab = lambda a: a.reshape(a.shape[:-1] + (N_SLABS, SLAB))
    per_dir = [slab(mu[:, :W_TOK]), slab(mu[:, W_TOK:2 * W_TOK]), slab(mu[:, 2 * W_TOK:3 * W_TOK]),
               slab(w0), slab(a0)]
    shared = [slab(z.reshape(1, W_TOK)) for z in (k_k, k_a, r_k, ln_g, ln_b)]
    rows = per_dir + [jnp.broadcast_to(z, (2, N_SLABS, SLAB)) for z in shared]
    prm = jnp.stack(rows, axis=2)
    prm = jnp.pad(prm, ((0, 0), (0, 0), (0, _N_PROWS - prm.shape[2]), (0, 0)))
    prm = jnp.transpose(prm, (1, 0, 2, 3)).astype(F32)
    mu_wa = mu[:, 3 * W_TOK:].astype(F32)
    wu = jnp.transpose(slab(w_up), (2, 0, 1, 3))
    au = jnp.transpose(slab(a_up), (2, 0, 1, 3))
    wcat = jnp.concatenate([jnp.concatenate([wu, jnp.zeros_like(wu)], axis=-1),
                            jnp.concatenate([jnp.zeros_like(au), au], axis=-1)], axis=2).astype(BF16)
    n_wa = RANK_W + RANK_A
    return pl.pallas_call(
        functools.partial(_rwkv_kernel, t=t),
        grid=(bsz, N_SLABS),
        in_specs=[
            pl.BlockSpec((1, t, SLAB), lambda b, g: (b, 0, g)),
            pl.BlockSpec((1, t, SLAB), lambda b, g: (b, 0, N_SLABS + g)),
            pl.BlockSpec((1, t, SLAB), lambda b, g: (b, 0, 2 * N_SLABS + g)),
            pl.BlockSpec((1, t, n_wa), lambda b, g: (b, 0, 3 * W_TOK // n_wa)),
            pl.BlockSpec((1, 2, _N_PROWS, SLAB), lambda b, g: (g, 0, 0, 0)),
            pl.BlockSpec((2, n_wa), lambda b, g: (0, 0)),
            pl.BlockSpec((1, 2, n_wa, 2 * SLAB), lambda b, g: (g, 0, 0, 0)),
        ],
        out_specs=pl.BlockSpec((1, t, SLAB), lambda b, g: (b, 0, g)),
        out_shape=jax.ShapeDtypeStruct((bsz, t, W_TOK), BF16),
        scratch_shapes=[pltpu.VMEM((t, SLAB), F32), pltpu.VMEM((t, SLAB), F32), pltpu.VMEM((2, SLAB, SLAB), F32)],
        compiler_params=_params("arbitrary", "arbitrary"),
        name="mixer_rwkv",
    )(p, p, p, p, prm, mu_wa, wcat)


def _trunk(x, mem, pre_norm, post_norm, mem_norm, w_mem_kv,
           a_w_in, a_conv_w, a_conv_b, a_ln_g, a_ln_b, a_w_out,
           b_w_in, b_mu, b_w0, b_w_up, b_a0, b_a_up, b_k_k, b_k_a, b_r_k, b_ln_g, b_ln_b, b_w_out,
           c_w_in, c_rpb, c_w_out):
    kt, mv = _mem_kv(mem, mem_norm, w_mem_kv)
    t = x.shape[1]
    tm = min(TOKEN_TILE, t)
    for l in range(DEPTH):
        kind, j = l % 3, l // 3
        if kind == 0:
            h, sg, mog = _in_proj(x, pre_norm[l], a_w_in[j], kt, mv, l, 2 * W_TOK, BF16, tm)
            x = _mixer_conv_out(h, a_conv_w[j], a_conv_b[j], a_ln_g[j], a_ln_b[j], sg, mog, x,
                                a_w_out[j], post_norm[l], min(CONV_TILE, t))
        elif kind == 1:
            h, sg, mog = _in_proj(x, pre_norm[l], b_w_in[j], kt, mv, l, P_B, BF16, tm)
            tok = _mixer_rwkv(h, b_mu[j], b_w0[j], b_w_up[j], b_a0[j], b_a_up[j],
                              b_k_k[j], b_k_a[j], b_r_k[j], b_ln_g[j], b_ln_b[j])
            x = _out_proj(tok, sg, mog, x, b_w_out[j], post_norm[l], tm)
        else:
            h, sg, mog = _in_proj(x, pre_norm[l], c_w_in[j], kt, mv, l, 3 * W_TOK, BF16, tm)
            tok = _mixer_natten(h, c_rpb[j], min(WIN_R, t // GRID_W))
            x = _out_proj(tok, sg, mog, x, c_w_out[j], post_norm[l], tm)
    return x


def kernel(x_prompt, x_sample, mem_prompt, mem_sample, pre_norm, post_norm, mem_norm, w_mem_kv,
           a_w_in, a_conv_w, a_conv_b, a_ln_g, a_ln_b, a_w_out,
           b_w_in, b_mu, b_w0, b_w_up, b_a0, b_a_up, b_k_k, b_k_a, b_r_k, b_ln_g, b_ln_b, b_w_out,
           c_w_in, c_rpb, c_w_out):
    weights = (pre_norm, post_norm, mem_norm, w_mem_kv,
               a_w_in, a_conv_w, a_conv_b, a_ln_g, a_ln_b, a_w_out,
               b_w_in, b_mu, b_w0, b_w_up, b_a0, b_a_up, b_k_k, b_k_a, b_r_k, b_ln_g, b_ln_b, b_w_out,
               c_w_in, c_rpb, c_w_out)
    return (_trunk(x_prompt, mem_prompt, *weights), _trunk(x_sample, mem_sample, *weights))
```

```python
import functools
import math

import numpy as np
import jax
import jax.numpy as jnp
from jax import lax
from jax.experimental import pallas as pl
from jax.experimental.pallas import tpu as pltpu

D_MODEL = 1024
DEPTH = 4
W_TOK = D_MODEL
N_MEM = 256
MEM_HEADS = 4
MEM_HEAD_DIM = 64
W_MEM = MEM_HEADS * MEM_HEAD_DIM
W_BRANCH = W_TOK + W_MEM
CONV_WIDTH = 31
CONV_HALO = 16
HEAD_DIM = 64
RANK_W = 64
RANK_A = 64
P_B = 3 * W_TOK + RANK_W + RANK_A
GRID_W = 64
WIN_R = 8
WIN_C = 16
RMS_EPS = 1e-6
LN_EPS = 1e-5
GN_EPS = 64e-5
DECAY_SCALE = math.exp(-0.5)
NEG_INF = -1e30

SUBLANES = 8
BF16_ROWS = 16
SLAB = 256
HEADS_PER_SLAB = SLAB // HEAD_DIM
N_SLABS = W_TOK // SLAB
CHUNK = 64
TOKEN_TILE = 512
CONV_TILE = 256
CONV_SUBTILES = 1
NATTEN_ROWS = 4
RWKV_UNROLL = 4
VMEM_LIMIT_BYTES = 56 * 1024 * 1024

F32 = jnp.float32
BF16 = jnp.bfloat16
NT_DIMS = (((1,), (1,)), ((), ()))


def _dot(a, b):
    return jnp.dot(a, b, preferred_element_type=F32)


def _dot_nt(a, b):
    return lax.dot_general(a, b, NT_DIMS, preferred_element_type=F32)


def _sigmoid(x):
    return 1.0 / (1.0 + jnp.exp(-x))


def _head_masks(width):
    lane = lax.broadcasted_iota(jnp.int32, (1, width), 1)
    return [(lane >= h * HEAD_DIM) & (lane < (h + 1) * HEAD_DIM) for h in range(width // HEAD_DIM)]


def _drain(gen):
    try:
        while True:
            next(gen)
    except StopIteration as stop:
        return stop.value


def _run_together(ga, gb):
    res = {}
    live = {key: g for key, g in (("a", ga), ("b", gb)) if g is not None}
    while live:
        for key in ("a", "b"):
            if key in live:
                try:
                    next(live[key])
                except StopIteration as stop:
                    res[key] = stop.value
                    del live[key]
    return res.get("a"), res.get("b")


def _params(*sem):
    return pltpu.CompilerParams(dimension_semantics=sem, vmem_limit_bytes=VMEM_LIMIT_BYTES)


def _mem_kv_kernel(mem_ref, g_ref, wkt_ref, wv_ref, kt_ref, v_ref):
    x = mem_ref[0]
    y = x * lax.rsqrt(jnp.mean(x * x, -1, keepdims=True) + RMS_EPS) * g_ref[0]
    yb = y.astype(BF16)
    kt_ref[0, 0] = _dot_nt(wkt_ref[0], yb).astype(BF16)
    v_ref[0, 0] = _dot(yb, wv_ref[0]).astype(BF16)


def _mem_kv(mem, mem_norm, w_mem_kv):
    bsz, m, d = mem.shape
    wkt = jnp.swapaxes(w_mem_kv[:, :, :W_MEM], 1, 2).astype(BF16)
    wv = w_mem_kv[:, :, W_MEM:].astype(BF16)
    return pl.pallas_call(
        _mem_kv_kernel,
        grid=(DEPTH, bsz),
        in_specs=[
            pl.BlockSpec((1, m, d), lambda l, b: (b, 0, 0)),
            pl.BlockSpec((1, 1, d), lambda l, b: (l, 0, 0)),
            pl.BlockSpec((1, W_MEM, d), lambda l, b: (l, 0, 0)),
            pl.BlockSpec((1, d, W_MEM), lambda l, b: (l, 0, 0)),
        ],
        out_specs=[
            pl.BlockSpec((1, 1, W_MEM, m), lambda l, b: (l, b, 0, 0)),
            pl.BlockSpec((1, 1, m, W_MEM), lambda l, b: (l, b, 0, 0)),
        ],
        out_shape=[
            jax.ShapeDtypeStruct((DEPTH, bsz, W_MEM, m), BF16),
            jax.ShapeDtypeStruct((DEPTH, bsz, m, W_MEM), BF16),
        ],
        compiler_params=_params("arbitrary", "arbitrary"),
        name="mem_kv",
    )(mem, mem_norm.reshape(DEPTH, 1, d), wkt, wv)


def _rms_norm_bf16(x, g):
    return (x * lax.rsqrt(jnp.mean(x * x, -1, keepdims=True) + RMS_EPS) * g).astype(BF16)


def _attend_and_gate(hn, w_ref, nc, kt, mv):
    q = _dot(hn, w_ref[:, nc:nc + W_MEM]) * (MEM_HEAD_DIM ** -0.5)
    gate = _dot(hn, w_ref[:, nc + W_MEM:])
    sgate = gate * _sigmoid(gate)
    masks = _head_masks(W_MEM)
    sc = [_dot(jnp.where(m, q, 0.0).astype(BF16), kt) for m in masks]
    e = [jnp.exp(s - jnp.max(s, -1, keepdims=True)) for s in sc]
    pr = [(x / jnp.sum(x, -1, keepdims=True)).astype(BF16) for x in e]
    o = [_dot(p, mv) for p in pr]
    mo = jnp.zeros(q.shape, F32)
    for m, x in zip(masks, o):
        mo = jnp.where(m, x, mo)
    return sgate[:, :W_TOK], mo * sgate[:, W_TOK:]


def _in_proj_kernel(x_ref, g_ref, w_ref, kt_ref, mv_ref, h_ref, sg_ref, mog_ref, *, nc):
    hn = _rms_norm_bf16(x_ref[0], g_ref[...])
    h_ref[0] = _dot(hn, w_ref[:, :nc]).astype(h_ref.dtype)
    sg, mog = _attend_and_gate(hn, w_ref, nc, kt_ref[0, 0], mv_ref[0, 0])
    sg_ref[0] = sg.astype(sg_ref.dtype)
    mog_ref[0] = mog.astype(mog_ref.dtype)


def _in_proj(x, g, w_in, kt, mv, layer, nc, h_dtype, tm):
    bsz, t, d = x.shape
    n_in = w_in.shape[1]
    m = kt.shape[-1]
    return pl.pallas_call(
        functools.partial(_in_proj_kernel, nc=nc),
        grid=(bsz, t // tm),
        in_specs=[
            pl.BlockSpec((1, tm, d), lambda b, j: (b, j, 0)),
            pl.BlockSpec((1, d), lambda b, j: (0, 0)),
            pl.BlockSpec((d, n_in), lambda b, j: (0, 0)),
            pl.BlockSpec((1, 1, W_MEM, m), lambda b, j: (layer, b, 0, 0)),
            pl.BlockSpec((1, 1, m, W_MEM), lambda b, j: (layer, b, 0, 0)),
        ],
        out_specs=[
            pl.BlockSpec((1, tm, nc), lambda b, j: (b, j, 0)),
            pl.BlockSpec((1, tm, W_TOK), lambda b, j: (b, j, 0)),
            pl.BlockSpec((1, tm, W_MEM), lambda b, j: (b, j, 0)),
        ],
        out_shape=[
            jax.ShapeDtypeStruct((bsz, t, nc), h_dtype),
            jax.ShapeDtypeStruct((bsz, t, W_TOK), BF16),
            jax.ShapeDtypeStruct((bsz, t, W_MEM), BF16),
        ],
        compiler_params=_params("arbitrary", "arbitrary"),
        name="in_proj",
    )(x, g.reshape(1, d), w_in.astype(BF16), kt, mv)


def _gate_project(tok, sg, mog, x, w_ref, g):
    o = _dot((tok * sg).astype(BF16), w_ref[:W_TOK, :]) + _dot(mog.astype(BF16), w_ref[W_TOK:, :])
    return x + o * lax.rsqrt(jnp.mean(o * o, -1, keepdims=True) + RMS_EPS) * g


def _out_proj_kernel(tok_ref, sg_ref, mog_ref, x_ref, w_ref, g_ref, o_ref):
    o_ref[0] = _gate_project(tok_ref[0].astype(F32), sg_ref[0], mog_ref[0], x_ref[0], w_ref, g_ref[...])


def _out_proj(tok, sg, mog, x, w_out, g, tm):
    bsz, t, d = x.shape
    return pl.pallas_call(
        _out_proj_kernel,
        grid=(bsz, t // tm),
        in_specs=[
            pl.BlockSpec((1, tm, W_TOK), lambda b, j: (b, j, 0)),
            pl.BlockSpec((1, tm, W_TOK), lambda b, j: (b, j, 0)),
            pl.BlockSpec((1, tm, W_MEM), lambda b, j: (b, j, 0)),
            pl.BlockSpec((1, tm, d), lambda b, j: (b, j, 0)),
            pl.BlockSpec((W_BRANCH, d), lambda b, j: (0, 0)),
            pl.BlockSpec((1, d), lambda b, j: (0, 0)),
        ],
        out_specs=pl.BlockSpec((1, tm, d), lambda b, j: (b, j, 0)),
        out_shape=jax.ShapeDtypeStruct((bsz, t, d), F32),
        compiler_params=_params("arbitrary", "arbitrary"),
        name="out_proj",
    )(tok, sg, mog, x, w_out.astype(BF16), g.reshape(1, d))


def _conv_layer_kernel(x_ref, prev_ref, next_ref, gpre_ref, win_ref, kt_ref, mv_ref, cw_ref, cb_ref, lg_ref, lb_ref,
                       wout_ref, gpost_ref, o_ref, zp_ref, q_ref, *, tt, n_sub):
    j = pl.program_id(1)
    nj = pl.num_programs(1)
    nc = 2 * W_TOK
    glu = lambda h: h[:, :W_TOK] * _sigmoid(h[:, W_TOK:])
    glu_of = lambda rows: glu(_dot(_rms_norm_bf16(rows, gpre_ref[...]), win_ref[:, :nc]))

    def project(i):
        lo = i * tt
        x = x_ref[0, lo:lo + tt, :]
        hn = _rms_norm_bf16(x, gpre_ref[...])
        zp_ref[i, CONV_HALO:CONV_HALO + tt, :] = glu(_dot(hn, win_ref[:, :nc]))
        yield
        if i == 0:
            before = glu_of(prev_ref[0]) * (j > 0).astype(F32)
        else:
            before = glu_of(x_ref[0, lo - CONV_HALO:lo, :])
        if i == n_sub - 1:
            after = glu_of(next_ref[0]) * (j < nj - 1).astype(F32)
        else:
            after = glu_of(x_ref[0, lo + tt:lo + tt + CONV_HALO, :])
        zp_ref[i, 0:CONV_HALO, :] = before
        zp_ref[i, CONV_HALO + tt:, :] = after
        yield
        sg, mog = _attend_and_gate(hn, win_ref, nc, kt_ref[0, 0], mv_ref[0, 0])
        return x, sg, mog

    def convolve(i):
        off = CONV_HALO - CONV_WIDTH // 2
        n = tt + SUBLANES
        acc = None
        for s in range(SUBLANES):
            q = None
            for a in range((CONV_WIDTH + off - 1) // SUBLANES + 1):
                tap = SUBLANES * a + s - off
                if 0 <= tap < CONV_WIDTH:
                    term = cw_ref[tap:tap + 1, :] * zp_ref[i, SUBLANES * a:SUBLANES * a + n, :]
                    q = term if q is None else q + term
            q_ref[i] = q
            part = q_ref[i, s:s + tt, :]
            acc = part if acc is None else acc + part
            yield
        z = acc + cb_ref[...]
        mean = jnp.mean(z, -1, keepdims=True)
        zc = z - mean
        var = jnp.mean(zc * zc, -1, keepdims=True)
        y = zc * lax.rsqrt(var + LN_EPS) * lg_ref[...] + lb_ref[...]
        return y * _sigmoid(y)

    def finish(i, tok, x, sg, mog):
        o_ref[0, i * tt:(i + 1) * tt, :] = _gate_project(tok, sg, mog, x, wout_ref, gpost_ref[...])
        yield

    gated = _drain(project(0))
    for i in range(n_sub):
        nxt = project(i + 1) if i + 1 < n_sub else None
        tok, gated_next = _run_together(convolve(i), nxt)
        _drain(finish(i, tok, *gated))
        gated = gated_next


def _conv_layer(x, g_pre, w_in, kt, mv, layer, conv_w, conv_b, ln_g, ln_b, w_out, g_post, tt):
    bsz, t, d = x.shape
    n_in = w_in.shape[1]
    m = kt.shape[-1]
    n_sub = min(CONV_SUBTILES, t // tt)
    step = n_sub * tt
    nh = t // CONV_HALO
    per = step // CONV_HALO
    vec = lambda a: a.reshape(1, -1)
    tile = lambda width: pl.BlockSpec((1, step, width), lambda b, j: (b, j, 0))
    whole = lambda shape: pl.BlockSpec(shape, lambda b, j: (0,) * len(shape))
    return pl.pallas_call(
        functools.partial(_conv_layer_kernel, tt=tt, n_sub=n_sub),
        grid=(bsz, t // step),
        in_specs=[
            tile(d),
            pl.BlockSpec((1, CONV_HALO, d), lambda b, j: (b, jnp.maximum(j * per - 1, 0), 0)),
            pl.BlockSpec((1, CONV_HALO, d), lambda b, j: (b, jnp.minimum((j + 1) * per, nh - 1), 0)),
            whole((1, d)), whole((d, n_in)),
            pl.BlockSpec((1, 1, W_MEM, m), lambda b, j: (layer, b, 0, 0)),
            pl.BlockSpec((1, 1, m, W_MEM), lambda b, j: (layer, b, 0, 0)),
            whole((CONV_WIDTH, W_TOK)), whole((1, W_TOK)), whole((1, W_TOK)), whole((1, W_TOK)),
            whole((W_BRANCH, d)), whole((1, d)),
        ],
        out_specs=tile(d),
        out_shape=jax.ShapeDtypeStruct((bsz, t, d), F32),
        scratch_shapes=[pltpu.VMEM((n_sub, tt + 2 * CONV_HALO, W_TOK), F32),
                        pltpu.VMEM((n_sub, tt + SUBLANES, W_TOK), F32)],
        compiler_params=_params("arbitrary", "arbitrary"),
        name="conv_layer",
    )(x, x, x, vec(g_pre), w_in.astype(BF16), kt, mv, conv_w, vec(conv_b), vec(ln_g), vec(ln_b),
      w_out.astype(BF16), vec(g_post))


def _natten_kernel(q_ref, k_ref, v_ref, bias_ref, o_ref, *, rows, rq):
    j = pl.program_id(2)
    masks = _head_masks(SLAB)
    kr = WIN_R * GRID_W

    def group(idx):
        r = [j * rq + i for i in idx]
        s = [jnp.clip(x - WIN_R // 2, 0, rows - WIN_R) for x in r]
        start = [pl.multiple_of(x * GRID_W, GRID_W) for x in s]
        q = [q_ref[0, i * GRID_W:(i + 1) * GRID_W, :] * (HEAD_DIM ** -0.5) for i in idx]
        qs = [jnp.concatenate([jnp.where(m, x, 0).astype(BF16) for m in masks], axis=0) for x in q]
        sc = [_dot_nt(x, k_ref[0, pl.ds(st, kr), :]) + bias_ref[ri - si, 0]
              for x, st, ri, si in zip(qs, start, r, s)]
        e = [jnp.exp(x - jnp.max(x, -1, keepdims=True)) for x in sc]
        pr = [(x / jnp.sum(x, -1, keepdims=True)).astype(BF16) for x in e]
        o = [_dot(x, v_ref[0, pl.ds(st, kr), :]) for x, st in zip(pr, start)]
        for i, x in zip(idx, o):
            out = jnp.zeros((GRID_W, SLAB), F32)
            for h, m in enumerate(masks):
                out = jnp.where(m, x[h * GRID_W:(h + 1) * GRID_W, :], out)
            o_ref[0, i * GRID_W:(i + 1) * GRID_W, :] = out.astype(o_ref.dtype)

    for g0 in range(0, rq, NATTEN_ROWS):
        group(list(range(g0, min(g0 + NATTEN_ROWS, rq))))


def _natten_bias(rpb):
    cols = np.arange(GRID_W)
    col_start = np.clip(cols - WIN_C // 2, 0, GRID_W - WIN_C)
    col_mask = (cols[None, :] >= col_start[:, None]) & (cols[None, :] < col_start[:, None] + WIN_C)
    col_idx = np.clip(cols[None, :] - cols[:, None], -(WIN_C - 1), WIN_C - 1) + WIN_C - 1
    rpb_cols = rpb[:, :, col_idx]
    ridx = np.arange(WIN_R)[None, :] + WIN_R - 1 - np.arange(WIN_R)[:, None]
    tab = rpb_cols[:, ridx]
    tab = jnp.where(col_mask[None, None, None], tab, NEG_INF)
    tab = jnp.transpose(tab, (1, 0, 3, 2, 4))
    return tab.reshape(WIN_R, N_SLABS, HEADS_PER_SLAB * GRID_W, WIN_R * GRID_W).astype(F32)


def _mixer_natten(qkv, rpb, rq):
    bsz, t, _ = qkv.shape
    rows = t // GRID_W
    tq = rq * GRID_W
    bias = _natten_bias(rpb)
    return pl.pallas_call(
        functools.partial(_natten_kernel, rows=rows, rq=rq),
        grid=(bsz, N_SLABS, t // tq),
        in_specs=[
            pl.BlockSpec((1, tq, SLAB), lambda b, g, j: (b, j, g)),
            pl.BlockSpec((1, t, SLAB), lambda b, g, j: (b, 0, N_SLABS + g)),
            pl.BlockSpec((1, t, SLAB), lambda b, g, j: (b, 0, 2 * N_SLABS + g)),
            pl.BlockSpec((WIN_R, 1, HEADS_PER_SLAB * GRID_W, WIN_R * GRID_W), lambda b, g, j: (0, g, 0, 0)),
        ],
        out_specs=pl.BlockSpec((1, tq, SLAB), lambda b, g, j: (b, j, g)),
        out_shape=jax.ShapeDtypeStruct((bsz, t, W_TOK), BF16),
        compiler_params=_params("arbitrary", "arbitrary", "arbitrary"),
        name="mixer_natten",
    )(qkv, qkv, qkv, bias)


_R_MU_R, _R_MU_K, _R_MU_V, _R_W0, _R_A0, _R_KG, _R_KA, _R_RK, _R_LG, _R_LB = range(10)
_N_PROWS = 16


def _stack_heads(z, masks):
    return jnp.concatenate([jnp.where(m, z, 0.0) for m in masks], axis=0)


def _collapse_heads(zs, c):
    out = zs[0:c]
    for h in range(1, HEADS_PER_SLAB):
        out = out + zs[h * c:(h + 1) * c]
    return out


def _split_dot(lhs_bf16, x):
    hi = x.astype(BF16)
    lo = (x - hi.astype(F32)).astype(BF16)
    return _dot(lhs_bf16, hi) + _dot(lhs_bf16, lo)


def _seg_sum(x, bd):
    c = x.shape[0]
    hi = x.astype(BF16)
    lo = (x - hi.astype(F32)).astype(BF16)
    s = _dot(jnp.concatenate([hi, lo], axis=0), bd)
    return s[:c] + s[c:]


def _rwkv_kernel(pr_ref, pk_ref, pv_ref, pwa_ref, prm_ref, muwa_ref, wcat_ref, o_ref,
                 y_ref, bonus_ref, s_ref, hb_ref, hf_ref, hd_ref, *, t):
    c = CHUNK
    n_chunks = t // c
    masks = _head_masks(SLAB)
    row = lax.broadcasted_iota(jnp.int32, (SLAB, SLAB), 0)
    col = lax.broadcasted_iota(jnp.int32, (SLAB, SLAB), 1)
    same_head = (row // HEAD_DIM) == (col // HEAD_DIM)
    bd_ones = same_head.astype(BF16)
    eye = (row == col).astype(F32)
    crow = lax.broadcasted_iota(jnp.int32, (c, c), 0)
    ccol = lax.broadcasted_iota(jnp.int32, (c, c), 1)
    rid = lax.broadcasted_iota(jnp.int32, (c, 1), 0)
    lane_wa = lax.broadcasted_iota(jnp.int32, (1, RANK_W + RANK_A), 1)

    cum_tri = [(crow >= ccol).astype(BF16), (crow <= ccol).astype(BF16)]
    strict = [row > col, row < col]
    incl = [row >= col, row <= col]
    edge = [0, c - 1]
    shift = [1, c - 1]

    def prow(d, i):
        return prm_ref[0, d, i:i + 1, :]

    far_masks = masks[2:] + masks[:2]

    def swap_halves(z):
        return jnp.concatenate([z[:, SLAB // 2:], z[:, :SLAB // 2]], axis=1)

    def pick(zs, lane_masks):
        out = jnp.where(lane_masks[0], zs[0:c], 0.0)
        for h in range(1, HEADS_PER_SLAB):
            out = out + jnp.where(lane_masks[h], zs[h * c:(h + 1) * c], 0.0)
        return out

    def stage_a(lanes):
        ds = [d for _, d in lanes]
        each = lambda f, *cols: [f(*args) for args in zip(*cols)]
        c0s = [pl.multiple_of(ci * c, c) for ci, _ in lanes]
        nidx = [pl.multiple_of(jnp.minimum(c0 + c, t - BF16_ROWS) if d else jnp.maximum(c0 - BF16_ROWS, 0),
                               BF16_ROWS) for c0, d in zip(c0s, ds)]
        valid = [((c0 + c < t) if d else (c0 > 0)).astype(F32) for c0, d in zip(c0s, ds)]

        def lerp(ref, mus):
            def one(c0, ni, va, d, mu):
                x = ref[0, pl.ds(c0, c), :].astype(F32)
                blk = ref[0, pl.ds(ni, BF16_ROWS), :].astype(F32)
                nb = (blk[0:1] if d else blk[BF16_ROWS - 1:BF16_ROWS]) * va
                sh = jnp.where(rid == edge[d], nb, pltpu.roll(x, shift[d], 0))
                return x + mu * (sh - x)
            return each(one, c0s, nidx, valid, ds, mus)

        r = lerp(pr_ref, [prow(d, _R_MU_R) for d in ds])
        k = lerp(pk_ref, [prow(d, _R_MU_K) for d in ds])
        v = lerp(pv_ref, [prow(d, _R_MU_V) for d in ds])
        wa = lerp(pwa_ref, [muwa_ref[d:d + 1, :] for d in ds])
        yield
        lora = each(lambda x, d: _dot(jnp.where(lane_wa < RANK_W, jnp.tanh(x), x).astype(BF16), wcat_ref[0, d]),
                    wa, ds)
        lw = each(lambda lo, d: -DECAY_SCALE * _sigmoid(prow(d, _R_W0) + lo[:, :SLAB]), lora, ds)
        a = each(lambda lo, d: _sigmoid(prow(d, _R_A0) + lo[:, SLAB:]), lora, ds)
        kk = each(lambda x: x * prow(0, _R_KG), k)
        ss = each(lambda x: _seg_sum(x * x, bd_ones), kk)
        yield
        kk = each(lambda x, s: x * lax.rsqrt(jnp.maximum(s, 1e-24)), kk, ss)
        keff =each(lambda x, aa: x * (1.0 + (aa - 1.0) * prow(0, _R_KA)), k, a)
        b = each(lambda x, aa: x * aa, kk, a)
        rk = each(lambda x, y: _seg_sum(x * y * prow(0, _R_RK), bd_ones), r, keff)
        bonus = each(lambda x, y: x * y, rk, v)
        yield
        cum = each(lambda x, d: _split_dot(cum_tri[d], x), lw, ds)
        yield
        tot =each(lambda x, d: x[0:1] if d else x[c - 1:c], cum, ds)
        e_neg = each(lambda x: jnp.exp(-x), cum)
        e_tail = each(lambda x, y: jnp.exp(x - y), tot, cum)
        rt = each(lambda x, y: x * jnp.exp(y), r, cum)
        kkt = each(lambda x, y, z: x * jnp.exp(y - z), kk, cum, lw)
        khat = each(lambda x, y: x * y, keff, e_neg)
        bhat = each(lambda x, y: x * y, b, e_neg)
        kb = each(lambda x, y, e: jnp.concatenate([x * e, -(y * e)], axis=0).astype(BF16), keff, b, e_tail)
        yield
        lhs =each(lambda x, y: jnp.concatenate([_stack_heads(x, masks), _stack_heads(y, masks)],
                                                axis=0).astype(BF16), kkt, rt)
        rhs = each(lambda x, y: jnp.concatenate([_stack_heads(x, masks), _stack_heads(y, masks)],
                                                axis=0).astype(BF16), khat, bhat)
        amat = each(_dot_nt, lhs, rhs)
        yield
        akb =each(lambda m, d: jnp.where(strict[d], m[:SLAB, SLAB:], 0.0), amat, ds)
        a_v = each(lambda m, d: jnp.concatenate([jnp.where(strict[d], m[:SLAB, :SLAB], 0.0),
                                                 jnp.where(incl[d], m[SLAB:, :SLAB], 0.0)], axis=0).astype(BF16),
                   amat, ds)
        arb = each(lambda m, d: jnp.where(incl[d], m[SLAB:, SLAB:], 0.0).astype(BF16), amat, ds)
        av = each(lambda m, x: _dot(m, _stack_heads(x, masks).astype(BF16)), a_v, v)
        yield
        x = each(lambda m: eye - m, akb)
        p = each(lambda m: m.astype(BF16), akb)
        for _ in range(int(math.log2(c)) - 1):
            p32 = each(lambda m: _dot(m, m), p)
            p = each(lambda m: m.astype(BF16), p32)
            x = each(lambda xx, pp: xx + _dot(xx.astype(BF16), pp), x, p)
            yield
        xinv =each(lambda m: m.astype(BF16), x)
        yield
        mpack = each(lambda x, m: (_stack_heads(x, masks) + swap_halves(m[:SLAB])).astype(BF16), kkt, av)
        z_s = each(_dot, xinv, mpack)
        yield
        g_s = each(lambda ab, z: _dot(ab, z.astype(BF16)), arb, z_s)
        w1 = each(lambda z: pick(z, masks), z_s)
        w2 = each(lambda z: swap_halves(pick(z, far_masks)), z_s)
        yield
        rg = each(lambda x, g: x - pick(g, masks), rt, g_s)
        yconst = each(lambda m, g: _collapse_heads(m[SLAB:], c) - swap_halves(pick(g, far_masks)), av, g_s)
        lhs_b = each(lambda x, y: jnp.concatenate([x, y], axis=0).astype(BF16), rg, w1)
        return [dict(c0=c0s[i], d=ds[i], lhs=lhs_b[i], yconst=yconst[i], w2=w2[i], kb=kb[i], v=v[i],
                     decay=jnp.exp(tot[i]), bonus=bonus[i]) for i in range(len(lanes))]

    def stage_b(la, states):
        each = lambda f, *cols: [f(*args) for args in zip(*cols)]
        ks = each(lambda l, s: _dot_nt(l["lhs"](), s.astype(BF16)), la, states)
        yield
        y = each(lambda l, x: x[:c] + l["yconst"](), la, ks)
        vs_t = each(lambda l, x: jnp.concatenate([l["v"](), x[c:] + l["w2"]()], axis=0).T.astype(BF16), la, ks)
        yield
        upd = each(lambda l, x: jnp.where(same_head, _dot(x, l["kb"]()), 0.0), la, vs_t)
        yield
        return y, each(lambda l, s, x: s * l["decay"]() + x, la, states, upd)

    s_ref[...] = jnp.zeros((2, SLAB, SLAB), F32)
    u = RWKV_UNROLL
    n_groups = n_chunks // u

    def group_lanes(j):
        return [(j * u + q, 0) for q in range(u)] + [(n_chunks - 1 - j * u - q, 1) for q in range(u)]

    drain, run_together = _drain, _run_together

    def save(la):
        for i, l in enumerate(la):
            hb_ref[i, 0] = l["lhs"]
            hb_ref[i, 1] = l["kb"]
            for slot, key in enumerate(("yconst", "w2", "v", "bonus")):
                hf_ref[i, slot] = l[key]
            hd_ref[i] = l["decay"] + jnp.zeros((SUBLANES, SLAB), F32)

    def load(j):
        def lane(i, ci, d):
            get = lambda ref, *idx: (lambda: ref[idx])
            return dict(c0=pl.multiple_of(ci * c, c), d=d, lhs=get(hb_ref, i, 0), kb=get(hb_ref, i, 1),
                        yconst=get(hf_ref, i, 0), w2=get(hf_ref, i, 1), v=get(hf_ref, i, 2),
                        bonus=get(hf_ref, i, 3), decay=get(hd_ref, i, slice(0, 1)))
        return [lane(i, ci, d) for i, (ci, d) in enumerate(group_lanes(j))]

    def stage_b_group(la):
        out = []
        states = [s_ref[0], s_ref[1]]
        for q in range(u):
            pair = [la[q], la[u + q]]
            ys, states = yield from stage_b(pair, states)
            out += list(zip(pair, ys))
        s_ref[0], s_ref[1] = states
        return out

    def emit(res, finish):
        if not finish:
            for l, y in res:
                y_ref[pl.ds(l["c0"], c), :] = y
                bonus_ref[pl.ds(l["c0"], c), :] = l["bonus"]()
            return
        ys = [y + y_ref[pl.ds(l["c0"], c), :] for l, y in res]
        mean = [_seg_sum(y, bd_ones) * (1.0 / HEAD_DIM) for y in ys]
        yc = [y - m for y, m in zip(ys, mean)]
        var = [_seg_sum(z * z, bd_ones) * (1.0 / HEAD_DIM) for z in yc]
        for (l, _), z, vv in zip(res, yc, var):
            out = z * lax.rsqrt(vv + GN_EPS) * prow(0, _R_LG) + prow(0, _R_LB)
            o_ref[0, pl.ds(l["c0"], c), :] = (out + l["bonus"]() + bonus_ref[pl.ds(l["c0"], c), :]).astype(o_ref.dtype)

    def step(j, finish):
        res = drain(stage_b_group(load(j)))
        la_next = drain(stage_a(group_lanes(j + 1)))
        emit(res, finish)
        save(la_next)

    save(drain(stage_a(group_lanes(0))))
    lax.fori_loop(0, n_groups // 2, lambda j, carry: (step(j, False), carry)[1], 0)
    lax.fori_loop(n_groups // 2, n_groups - 1, lambda j, carry: (step(j, True), carry)[1], 0)
    emit(drain(stage_b_group(load(n_groups - 1))), True)


def _mixer_rwkv(p, mu, w0, w_up, a0, a_up, k_k, k_a, r_k, ln_g, ln_b):
    bsz, t, _ = p.shape
    assert t % (2 * CHUNK * RWKV_UNROLL) == 0
    n_lanes = 2 * RWKV_UNROLL
    slab = lambda a: a.reshape(a.shape[:-1] + (N_SLABS, SLAB))
    per_dir = [slab(mu[:, :W_TOK]), slab(mu[:, W_TOK:2 * W_TOK]), slab(mu[:, 2 * W_TOK:3 * W_TOK]),
               slab(w0), slab(a0)]
    shared = [slab(z.reshape(1, W_TOK)) for z in (k_k, k_a, r_k, ln_g, ln_b)]
    rows = per_dir + [jnp.broadcast_to(z, (2, N_SLABS, SLAB)) for z in shared]
    prm = jnp.stack(rows, axis=2)
    prm = jnp.pad(prm, ((0, 0), (0, 0), (0, _N_PROWS - prm.shape[2]), (0, 0)))
    prm = jnp.transpose(prm, (1, 0, 2, 3)).astype(F32)
    mu_wa = mu[:, 3 * W_TOK:].astype(F32)
    wu = jnp.transpose(slab(w_up), (2, 0, 1, 3))
    au = jnp.transpose(slab(a_up), (2, 0, 1, 3))
    wcat = jnp.concatenate([jnp.concatenate([wu, jnp.zeros_like(wu)], axis=-1),
                            jnp.concatenate([jnp.zeros_like(au), au], axis=-1)], axis=2).astype(BF16)
    n_wa = RANK_W + RANK_A
    return pl.pallas_call(
        functools.partial(_rwkv_kernel, t=t),
        grid=(bsz, N_SLABS),
        in_specs=[
            pl.BlockSpec((1, t, SLAB), lambda b, g: (b, 0, g)),
            pl.BlockSpec((1, t, SLAB), lambda b, g: (b, 0, N_SLABS + g)),
            pl.BlockSpec((1, t, SLAB), lambda b, g: (b, 0, 2 * N_SLABS + g)),
            pl.BlockSpec((1, t, n_wa), lambda b, g: (b, 0, 3 * W_TOK // n_wa)),
            pl.BlockSpec((1, 2, _N_PROWS, SLAB), lambda b, g: (g, 0, 0, 0)),
            pl.BlockSpec((2, n_wa), lambda b, g: (0, 0)),
            pl.BlockSpec((1, 2, n_wa, 2 * SLAB), lambda b, g: (g, 0, 0, 0)),
        ],
        out_specs=pl.BlockSpec((1, t, SLAB), lambda b, g: (b, 0, g)),
        out_shape=jax.ShapeDtypeStruct((bsz, t, W_TOK), BF16),
        scratch_shapes=[pltpu.VMEM((t, SLAB), F32), pltpu.VMEM((t, SLAB), F32), pltpu.VMEM((2, SLAB, SLAB), F32),
                        pltpu.VMEM((n_lanes, 2, 2 * CHUNK, SLAB), BF16), pltpu.VMEM((n_lanes, 4, CHUNK, SLAB), F32),
                        pltpu.VMEM((n_lanes, SUBLANES, SLAB), F32)],
        compiler_params=_params("arbitrary", "arbitrary"),
        name="mixer_rwkv",
    )(p, p, p, p, prm, mu_wa, wcat)


def _trunk(x, mem, pre_norm, post_norm, mem_norm, w_mem_kv,
           a_w_in, a_conv_w, a_conv_b, a_ln_g, a_ln_b, a_w_out,
           b_w_in, b_mu, b_w0, b_w_up, b_a0, b_a_up, b_k_k, b_k_a, b_r_k, b_ln_g, b_ln_b, b_w_out,
           c_w_in, c_rpb, c_w_out):
    kt, mv = _mem_kv(mem, mem_norm, w_mem_kv)
    t = x.shape[1]
    tm = min(TOKEN_TILE, t)
    for l in range(DEPTH):
        kind, j = l % 3, l // 3
        if kind == 0:
            x = _conv_layer(x, pre_norm[l], a_w_in[j], kt, mv, l, a_conv_w[j], a_conv_b[j], a_ln_g[j], a_ln_b[j],
                            a_w_out[j], post_norm[l], min(CONV_TILE, t))
        elif kind == 1:
            h, sg, mog = _in_proj(x, pre_norm[l], b_w_in[j], kt, mv, l, P_B, BF16, tm)
            tok = _mixer_rwkv(h, b_mu[j], b_w0[j], b_w_up[j], b_a0[j], b_a_up[j],
                              b_k_k[j], b_k_a[j], b_r_k[j], b_ln_g[j], b_ln_b[j])
            x = _out_proj(tok, sg, mog, x, b_w_out[j], post_norm[l], tm)
        else:
            h, sg, mog = _in_proj(x, pre_norm[l], c_w_in[j], kt, mv, l, 3 * W_TOK, BF16, tm)
            tok = _mixer_natten(h, c_rpb[j], min(WIN_R, t // GRID_W))
            x = _out_proj(tok, sg, mog, x, c_w_out[j], post_norm[l], tm)
    return x


def kernel(x_prompt, x_sample, mem_prompt, mem_sample, pre_norm, post_norm, mem_norm, w_mem_kv,
           a_w_in, a_conv_w, a_conv_b, a_ln_g, a_ln_b, a_w_out,
           b_w_in, b_mu, b_w0, b_w_up, b_a0, b_a_up, b_k_k, b_k_a, b_r_k, b_ln_g, b_ln_b, b_w_out,
           c_w_in, c_rpb, c_w_out):
    weights = (pre_norm, post_norm, mem_norm, w_mem_kv,
               a_w_in, a_conv_w, a_conv_b, a_ln_g, a_ln_b, a_w_out,
               b_w_in, b_mu, b_w0, b_w_up, b_a0, b_a_up, b_k_k, b_k_a, b_r_k, b_ln_g, b_ln_b, b_w_out,
               c_w_in, c_rpb, c_w_out)
    return (_trunk(x_prompt, mem_prompt, *weights), _trunk(x_sample, mem_sample, *weights))
```

```python
import functools
import math

import numpy as np
import jax
import jax.numpy as jnp
from jax import lax
from jax.experimental import pallas as pl
from jax.experimental.pallas import tpu as pltpu

D_MODEL = 1024
DEPTH = 4
W_TOK = D_MODEL
N_MEM = 256
MEM_HEADS = 4
MEM_HEAD_DIM = 64
W_MEM = MEM_HEADS * MEM_HEAD_DIM
W_BRANCH = W_TOK + W_MEM
CONV_WIDTH = 31
CONV_HALO = 16
HEAD_DIM = 64
RANK_W = 64
RANK_A = 64
P_B = 3 * W_TOK + RANK_W + RANK_A
GRID_W = 64
WIN_R = 8
WIN_C = 16
RMS_EPS = 1e-6
LN_EPS = 1e-5
GN_EPS = 64e-5
DECAY_SCALE = math.exp(-0.5)
NEG_INF = -1e30

SUBLANES = 8
BF16_ROWS = 16
SLAB = 256
HEADS_PER_SLAB = SLAB // HEAD_DIM
N_SLABS = W_TOK // SLAB
CHUNK = 64
TOKEN_TILE = 512
CONV_TILE = 256
CONV_SUBTILES = 1
NATTEN_ROWS = 8
RWKV_UNROLL = 4
VMEM_LIMIT_BYTES = 56 * 1024 * 1024

F32 = jnp.float32
BF16 = jnp.bfloat16
NT_DIMS = (((1,), (1,)), ((), ()))


def _dot(a, b):
    return jnp.dot(a, b, preferred_element_type=F32)


def _dot_nt(a, b):
    return lax.dot_general(a, b, NT_DIMS, preferred_element_type=F32)


def _sigmoid(x):
    return 0.5 * jnp.tanh(0.5 * x) + 0.5


def _head_masks(width):
    lane = lax.broadcasted_iota(jnp.int32, (1, width), 1)
    return [(lane >= h * HEAD_DIM) & (lane < (h + 1) * HEAD_DIM) for h in range(width // HEAD_DIM)]


def _drain(gen):
    try:
        while True:
            next(gen)
    except StopIteration as stop:
        return stop.value


def _run_together(ga, gb):
    res = {}
    live = {key: g for key, g in (("a", ga), ("b", gb)) if g is not None}
    while live:
        for key in ("a", "b"):
            if key in live:
                try:
                    next(live[key])
                except StopIteration as stop:
                    res[key] = stop.value
                    del live[key]
    return res.get("a"), res.get("b")


def _params(*sem):
    return pltpu.CompilerParams(dimension_semantics=sem, vmem_limit_bytes=VMEM_LIMIT_BYTES)


def _mem_kv_kernel(mem_ref, g_ref, wkt_ref, wv_ref, kt_ref, v_ref):
    x = mem_ref[0]
    y = x * lax.rsqrt(jnp.mean(x * x, -1, keepdims=True) + RMS_EPS) * g_ref[0]
    yb = y.astype(BF16)
    kt_ref[0, 0] = _dot_nt(wkt_ref[0], yb).astype(BF16)
    v_ref[0, 0] = _dot(yb, wv_ref[0]).astype(BF16)


def _mem_kv(mem, mem_norm, w_mem_kv):
    bsz, m, d = mem.shape
    wkt = jnp.swapaxes(w_mem_kv[:, :, :W_MEM], 1, 2).astype(BF16)
    wv = w_mem_kv[:, :, W_MEM:].astype(BF16)
    return pl.pallas_call(
        _mem_kv_kernel,
        grid=(DEPTH, bsz),
        in_specs=[
            pl.BlockSpec((1, m, d), lambda l, b: (b, 0, 0)),
            pl.BlockSpec((1, 1, d), lambda l, b: (l, 0, 0)),
            pl.BlockSpec((1, W_MEM, d), lambda l, b: (l, 0, 0)),
            pl.BlockSpec((1, d, W_MEM), lambda l, b: (l, 0, 0)),
        ],
        out_specs=[
            pl.BlockSpec((1, 1, W_MEM, m), lambda l, b: (l, b, 0, 0)),
            pl.BlockSpec((1, 1, m, W_MEM), lambda l, b: (l, b, 0, 0)),
        ],
        out_shape=[
            jax.ShapeDtypeStruct((DEPTH, bsz, W_MEM, m), BF16),
            jax.ShapeDtypeStruct((DEPTH, bsz, m, W_MEM), BF16),
        ],
        compiler_params=_params("arbitrary", "arbitrary"),
        name="mem_kv",
    )(mem, mem_norm.reshape(DEPTH, 1, d), wkt, wv)


def _rms_norm_bf16(x, g):
    return (x * lax.rsqrt(jnp.mean(x * x, -1, keepdims=True) + RMS_EPS) * g).astype(BF16)


def _attend_and_gate(hn, w_ref, nc, kt, mv):
    q = _dot(hn, w_ref[:, nc:nc + W_MEM]) * (MEM_HEAD_DIM ** -0.5)
    gate = _dot(hn, w_ref[:, nc + W_MEM:])
    sgate = gate * _sigmoid(gate)
    masks = _head_masks(W_MEM)
    sc = [_dot(jnp.where(m, q, 0.0).astype(BF16), kt) for m in masks]
    e = [jnp.exp(s - jnp.max(s, -1, keepdims=True)) for s in sc]
    pr = [(x / jnp.sum(x, -1, keepdims=True)).astype(BF16) for x in e]
    o = [_dot(p, mv) for p in pr]
    mo = jnp.zeros(q.shape, F32)
    for m, x in zip(masks, o):
        mo = jnp.where(m, x, mo)
    return sgate[:, :W_TOK], mo * sgate[:, W_TOK:]


def _in_proj_kernel(x_ref, g_ref, w_ref, kt_ref, mv_ref, h_ref, sg_ref, mog_ref, *, nc):
    hn = _rms_norm_bf16(x_ref[0], g_ref[...])
    h_ref[0] = _dot(hn, w_ref[:, :nc]).astype(h_ref.dtype)
    sg, mog = _attend_and_gate(hn, w_ref, nc, kt_ref[0, 0], mv_ref[0, 0])
    sg_ref[0] = sg.astype(sg_ref.dtype)
    mog_ref[0] = mog.astype(mog_ref.dtype)


def _in_proj(x, g, w_in, kt, mv, layer, nc, h_dtype, tm):
    bsz, t, d = x.shape
    n_in = w_in.shape[1]
    m = kt.shape[-1]
    return pl.pallas_call(
        functools.partial(_in_proj_kernel, nc=nc),
        grid=(bsz, t // tm),
        in_specs=[
            pl.BlockSpec((1, tm, d), lambda b, j: (b, j, 0)),
            pl.BlockSpec((1, d), lambda b, j: (0, 0)),
            pl.BlockSpec((d, n_in), lambda b, j: (0, 0)),
            pl.BlockSpec((1, 1, W_MEM, m), lambda b, j: (layer, b, 0, 0)),
            pl.BlockSpec((1, 1, m, W_MEM), lambda b, j: (layer, b, 0, 0)),
        ],
        out_specs=[
            pl.BlockSpec((1, tm, nc), lambda b, j: (b, j, 0)),
            pl.BlockSpec((1, tm, W_TOK), lambda b, j: (b, j, 0)),
            pl.BlockSpec((1, tm, W_MEM), lambda b, j: (b, j, 0)),
        ],
        out_shape=[
            jax.ShapeDtypeStruct((bsz, t, nc), h_dtype),
            jax.ShapeDtypeStruct((bsz, t, W_TOK), BF16),
            jax.ShapeDtypeStruct((bsz, t, W_MEM), BF16),
        ],
        compiler_params=_params("arbitrary", "arbitrary"),
        name="in_proj",
    )(x, g.reshape(1, d), w_in.astype(BF16), kt, mv)


def _gate_project(tok, sg, mog, x, w_ref, g):
    o = _dot((tok * sg).astype(BF16), w_ref[:W_TOK, :]) + _dot(mog.astype(BF16), w_ref[W_TOK:, :])
    return x + o * lax.rsqrt(jnp.mean(o * o, -1, keepdims=True) + RMS_EPS) * g


def _out_proj_kernel(tok_ref, sg_ref, mog_ref, x_ref, w_ref, g_ref, o_ref):
    o_ref[0] = _gate_project(tok_ref[0].astype(F32), sg_ref[0], mog_ref[0], x_ref[0], w_ref, g_ref[...])


def _out_proj(tok, sg, mog, x, w_out, g, tm):
    bsz, t, d = x.shape
    return pl.pallas_call(
        _out_proj_kernel,
        grid=(bsz, t // tm),
        in_specs=[
            pl.BlockSpec((1, tm, W_TOK), lambda b, j: (b, j, 0)),
            pl.BlockSpec((1, tm, W_TOK), lambda b, j: (b, j, 0)),
            pl.BlockSpec((1, tm, W_MEM), lambda b, j: (b, j, 0)),
            pl.BlockSpec((1, tm, d), lambda b, j: (b, j, 0)),
            pl.BlockSpec((W_BRANCH, d), lambda b, j: (0, 0)),
            pl.BlockSpec((1, d), lambda b, j: (0, 0)),
        ],
        out_specs=pl.BlockSpec((1, tm, d), lambda b, j: (b, j, 0)),
        out_shape=jax.ShapeDtypeStruct((bsz, t, d), F32),
        compiler_params=_params("arbitrary", "arbitrary"),
        name="out_proj",
    )(tok, sg, mog, x, w_out.astype(BF16), g.reshape(1, d))


def _conv_layer_kernel(x_ref, prev_ref, next_ref, gpre_ref, win_ref, kt_ref, mv_ref, cw_ref, cb_ref, lg_ref, lb_ref,
                       wout_ref, gpost_ref, o_ref, zp_ref, q_ref, *, tt, n_sub):
    j = pl.program_id(1)
    nj = pl.num_programs(1)
    nc = 2 * W_TOK
    glu = lambda h: h[:, :W_TOK] * _sigmoid(h[:, W_TOK:])
    glu_of = lambda rows: glu(_dot(_rms_norm_bf16(rows, gpre_ref[...]), win_ref[:, :nc]))

    def project(i):
        lo = i * tt
        x = x_ref[0, lo:lo + tt, :]
        hn = _rms_norm_bf16(x, gpre_ref[...])
        zp_ref[i, CONV_HALO:CONV_HALO + tt, :] = glu(_dot(hn, win_ref[:, :nc]))
        yield
        if i == 0:
            before = glu_of(prev_ref[0]) * (j > 0).astype(F32)
        else:
            before = glu_of(x_ref[0, lo - CONV_HALO:lo, :])
        if i == n_sub - 1:
            after = glu_of(next_ref[0]) * (j < nj - 1).astype(F32)
        else:
            after = glu_of(x_ref[0, lo + tt:lo + tt + CONV_HALO, :])
        zp_ref[i, 0:CONV_HALO, :] = before
        zp_ref[i, CONV_HALO + tt:, :] = after
        yield
        sg, mog = _attend_and_gate(hn, win_ref, nc, kt_ref[0, 0], mv_ref[0, 0])
        return x, sg, mog

    def convolve(i):
        off = CONV_HALO - CONV_WIDTH // 2
        n = tt + SUBLANES
        acc = None
        for s in range(SUBLANES):
            q = None
            for a in range((CONV_WIDTH + off - 1) // SUBLANES + 1):
                tap = SUBLANES * a + s - off
                if 0 <= tap < CONV_WIDTH:
                    term = cw_ref[tap:tap + 1, :] * zp_ref[i, SUBLANES * a:SUBLANES * a + n, :]
                    q = term if q is None else q + term
            q_ref[i] = q
            part = q_ref[i, s:s + tt, :]
            acc = part if acc is None else acc + part
            yield
        z = acc + cb_ref[...]
        mean = jnp.mean(z, -1, keepdims=True)
        zc = z - mean
        var = jnp.mean(zc * zc, -1, keepdims=True)
        y = zc * lax.rsqrt(var + LN_EPS) * lg_ref[...] + lb_ref[...]
        return y * _sigmoid(y)

    def finish(i, tok, x, sg, mog):
        o_ref[0, i * tt:(i + 1) * tt, :] = _gate_project(tok, sg, mog, x, wout_ref, gpost_ref[...])
        yield

    gated = _drain(project(0))
    for i in range(n_sub):
        nxt = project(i + 1) if i + 1 < n_sub else None
        tok, gated_next = _run_together(convolve(i), nxt)
        _drain(finish(i, tok, *gated))
        gated = gated_next


def _conv_layer(x, g_pre, w_in, kt, mv, layer, conv_w, conv_b, ln_g, ln_b, w_out, g_post, tt):
    bsz, t, d = x.shape
    n_in = w_in.shape[1]
    m = kt.shape[-1]
    n_sub = min(CONV_SUBTILES, t // tt)
    step = n_sub * tt
    nh = t // CONV_HALO
    per = step // CONV_HALO
    vec = lambda a: a.reshape(1, -1)
    tile = lambda width: pl.BlockSpec((1, step, width), lambda b, j: (b, j, 0))
    whole = lambda shape: pl.BlockSpec(shape, lambda b, j: (0,) * len(shape))
    return pl.pallas_call(
        functools.partial(_conv_layer_kernel, tt=tt, n_sub=n_sub),
        grid=(bsz, t // step),
        in_specs=[
            tile(d),
            pl.BlockSpec((1, CONV_HALO, d), lambda b, j: (b, jnp.maximum(j * per - 1, 0), 0)),
            pl.BlockSpec((1, CONV_HALO, d), lambda b, j: (b, jnp.minimum((j + 1) * per, nh - 1), 0)),
            whole((1, d)), whole((d, n_in)),
            pl.BlockSpec((1, 1, W_MEM, m), lambda b, j: (layer, b, 0, 0)),
            pl.BlockSpec((1, 1, m, W_MEM), lambda b, j: (layer, b, 0, 0)),
            whole((CONV_WIDTH, W_TOK)), whole((1, W_TOK)), whole((1, W_TOK)), whole((1, W_TOK)),
            whole((W_BRANCH, d)), whole((1, d)),
        ],
        out_specs=tile(d),
        out_shape=jax.ShapeDtypeStruct((bsz, t, d), F32),
        scratch_shapes=[pltpu.VMEM((n_sub, tt + 2 * CONV_HALO, W_TOK), F32),
                        pltpu.VMEM((n_sub, tt + SUBLANES, W_TOK), F32)],
        compiler_params=_params("arbitrary", "arbitrary"),
        name="conv_layer",
    )(x, x, x, vec(g_pre), w_in.astype(BF16), kt, mv, conv_w, vec(conv_b), vec(ln_g), vec(ln_b),
      w_out.astype(BF16), vec(g_post))


def _natten_kernel(q_ref, k_ref, v_ref, bias_ref, o_ref, *, rows, rq):
    j = pl.program_id(2)
    masks = _head_masks(SLAB)
    kr = WIN_R * GRID_W

    def group(idx):
        r = [j * rq + i for i in idx]
        s = [jnp.clip(x - WIN_R // 2, 0, rows - WIN_R) for x in r]
        start = [pl.multiple_of(x * GRID_W, GRID_W) for x in s]
        q = [q_ref[0, i * GRID_W:(i + 1) * GRID_W, :] * (HEAD_DIM ** -0.5) for i in idx]
        qs = [jnp.concatenate([jnp.where(m, x, 0).astype(BF16) for m in masks], axis=0) for x in q]
        sc = [_dot_nt(x, k_ref[0, pl.ds(st, kr), :]) + bias_ref[ri - si, 0]
              for x, st, ri, si in zip(qs, start, r, s)]
        e = [jnp.exp(x - jnp.max(x, -1, keepdims=True)) for x in sc]
        pr = [(x / jnp.sum(x, -1, keepdims=True)).astype(BF16) for x in e]
        o = [_dot(x, v_ref[0, pl.ds(st, kr), :]) for x, st in zip(pr, start)]
        for i, x in zip(idx, o):
            out = jnp.zeros((GRID_W, SLAB), F32)
            for h, m in enumerate(masks):
                out = jnp.where(m, x[h * GRID_W:(h + 1) * GRID_W, :], out)
            o_ref[0, i * GRID_W:(i + 1) * GRID_W, :] = out.astype(o_ref.dtype)

    for g0 in range(0, rq, NATTEN_ROWS):
        group(list(range(g0, min(g0 + NATTEN_ROWS, rq))))


def _natten_bias(rpb):
    cols = np.arange(GRID_W)
    col_start = np.clip(cols - WIN_C // 2, 0, GRID_W - WIN_C)
    col_mask = (cols[None, :] >= col_start[:, None]) & (cols[None, :] < col_start[:, None] + WIN_C)
    col_idx = np.clip(cols[None, :] - cols[:, None], -(WIN_C - 1), WIN_C - 1) + WIN_C - 1
    rpb_cols = rpb[:, :, col_idx]
    ridx = np.arange(WIN_R)[None, :] + WIN_R - 1 - np.arange(WIN_R)[:, None]
    tab = rpb_cols[:, ridx]
    tab = jnp.where(col_mask[None, None, None], tab, NEG_INF)
    tab = jnp.transpose(tab, (1, 0, 3, 2, 4))
    return tab.reshape(WIN_R, N_SLABS, HEADS_PER_SLAB * GRID_W, WIN_R * GRID_W).astype(F32)


def _mixer_natten(qkv, rpb, rq):
    bsz, t, _ = qkv.shape
    rows = t // GRID_W
    tq = rq * GRID_W
    bias = _natten_bias(rpb)
    return pl.pallas_call(
        functools.partial(_natten_kernel, rows=rows, rq=rq),
        grid=(bsz, N_SLABS, t // tq),
        in_specs=[
            pl.BlockSpec((1, tq, SLAB), lambda b, g, j: (b, j, g)),
            pl.BlockSpec((1, t, SLAB), lambda b, g, j: (b, 0, N_SLABS + g)),
            pl.BlockSpec((1, t, SLAB), lambda b, g, j: (b, 0, 2 * N_SLABS + g)),
            pl.BlockSpec((WIN_R, 1, HEADS_PER_SLAB * GRID_W, WIN_R * GRID_W), lambda b, g, j: (0, g, 0, 0)),
        ],
        out_specs=pl.BlockSpec((1, tq, SLAB), lambda b, g, j: (b, j, g)),
        out_shape=jax.ShapeDtypeStruct((bsz, t, W_TOK), BF16),
        compiler_params=_params("arbitrary", "arbitrary", "arbitrary"),
        name="mixer_natten",
    )(qkv, qkv, qkv, bias)


_R_MU_R, _R_MU_K, _R_MU_V, _R_W0, _R_A0, _R_KG, _R_KA, _R_RK, _R_LG, _R_LB = range(10)
_N_PROWS = 16


def _stack_heads(z, masks):
    return jnp.concatenate([jnp.where(m, z, 0.0) for m in masks], axis=0)


def _collapse_heads(zs, c):
    out = zs[0:c]
    for h in range(1, HEADS_PER_SLAB):
        out = out + zs[h * c:(h + 1) * c]
    return out


def _split_dot(lhs_bf16, x):
    hi = x.astype(BF16)
    lo = (x - hi.astype(F32)).astype(BF16)
    return _dot(lhs_bf16, hi) + _dot(lhs_bf16, lo)


def _seg_sum(x, bd):
    c = x.shape[0]
    hi = x.astype(BF16)
    lo = (x - hi.astype(F32)).astype(BF16)
    s = _dot(jnp.concatenate([hi, lo], axis=0), bd)
    return s[:c] + s[c:]


def _rwkv_kernel(pr_ref, pk_ref, pv_ref, pwa_ref, prm_ref, muwa_ref, wcat_ref, o_ref,
                 y_ref, bonus_ref, s_ref, hb_ref, hf_ref, hd_ref, *, t):
    c = CHUNK
    n_chunks = t // c
    masks = _head_masks(SLAB)
    row = lax.broadcasted_iota(jnp.int32, (SLAB, SLAB), 0)
    col = lax.broadcasted_iota(jnp.int32, (SLAB, SLAB), 1)
    same_head = (row // HEAD_DIM) == (col // HEAD_DIM)
    bd_ones = same_head.astype(BF16)
    eye = (row == col).astype(F32)
    crow = lax.broadcasted_iota(jnp.int32, (c, c), 0)
    ccol = lax.broadcasted_iota(jnp.int32, (c, c), 1)
    rid = lax.broadcasted_iota(jnp.int32, (c, 1), 0)
    lane_wa = lax.broadcasted_iota(jnp.int32, (1, RANK_W + RANK_A), 1)

    cum_tri = [(crow >= ccol).astype(BF16), (crow <= ccol).astype(BF16)]
    strict = [row > col, row < col]
    incl = [row >= col, row <= col]
    edge = [0, c - 1]
    shift = [1, c - 1]

    def prow(d, i):
        return prm_ref[0, d, i:i + 1, :]

    far_masks = masks[2:] + masks[:2]

    def swap_halves(z):
        return jnp.concatenate([z[:, SLAB // 2:], z[:, :SLAB // 2]], axis=1)

    def pick(zs, lane_masks):
        out = jnp.where(lane_masks[0], zs[0:c], 0.0)
        for h in range(1, HEADS_PER_SLAB):
            out = out + jnp.where(lane_masks[h], zs[h * c:(h + 1) * c], 0.0)
        return out

    def stage_a(lanes):
        ds = [d for _, d in lanes]
        each = lambda f, *cols: [f(*args) for args in zip(*cols)]
        c0s = [pl.multiple_of(ci * c, c) for ci, _ in lanes]
        nidx = [pl.multiple_of(jnp.minimum(c0 + c, t - BF16_ROWS) if d else jnp.maximum(c0 - BF16_ROWS, 0),
                               BF16_ROWS) for c0, d in zip(c0s, ds)]
        valid = [((c0 + c < t) if d else (c0 > 0)).astype(F32) for c0, d in zip(c0s, ds)]

        def lerp(ref, mus):
            def one(c0, ni, va, d, mu):
                x = ref[0, pl.ds(c0, c), :].astype(F32)
                blk = ref[0, pl.ds(ni, BF16_ROWS), :].astype(F32)
                nb = (blk[0:1] if d else blk[BF16_ROWS - 1:BF16_ROWS]) * va
                sh = jnp.where(rid == edge[d], nb, pltpu.roll(x, shift[d], 0))
                return x + mu * (sh - x)
            return each(one, c0s, nidx, valid, ds, mus)

        r = lerp(pr_ref, [prow(d, _R_MU_R) for d in ds])
        k = lerp(pk_ref, [prow(d, _R_MU_K) for d in ds])
        v = lerp(pv_ref, [prow(d, _R_MU_V) for d in ds])
        wa = lerp(pwa_ref, [muwa_ref[d:d + 1, :] for d in ds])
        yield
        lora = each(lambda x, d: _dot(jnp.where(lane_wa < RANK_W, jnp.tanh(x), x).astype(BF16), wcat_ref[0, d]),
                    wa, ds)
        lw = each(lambda lo, d: -DECAY_SCALE * _sigmoid(prow(d, _R_W0) + lo[:, :SLAB]), lora, ds)
        a = each(lambda lo, d: _sigmoid(prow(d, _R_A0) + lo[:, SLAB:]), lora, ds)
        kk = each(lambda x: x * prow(0, _R_KG), k)
        ss = each(lambda x: _seg_sum(x * x, bd_ones), kk)
        yield
        kk = each(lambda x, s: x * lax.rsqrt(jnp.maximum(s, 1e-24)), kk, ss)
        keff =each(lambda x, aa: x * (1.0 + (aa - 1.0) * prow(0, _R_KA)), k, a)
        b = each(lambda x, aa: x * aa, kk, a)
        rk = each(lambda x, y: _seg_sum(x * y * prow(0, _R_RK), bd_ones), r, keff)
        bonus = each(lambda x, y: x * y, rk, v)
        yield
        cum = each(lambda x, d: _split_dot(cum_tri[d], x), lw, ds)
        yield
        tot =each(lambda x, d: x[0:1] if d else x[c - 1:c], cum, ds)
        e_neg = each(lambda x: jnp.exp(-x), cum)
        e_tail = each(lambda x, y: jnp.exp(x - y), tot, cum)
        rt = each(lambda x, y: x * jnp.exp(y), r, cum)
        kkt = each(lambda x, y, z: x * jnp.exp(y - z), kk, cum, lw)
        khat = each(lambda x, y: x * y, keff, e_neg)
        bhat = each(lambda x, y: x * y, b, e_neg)
        kb = each(lambda x, y, e: jnp.concatenate([x * e, -(y * e)], axis=0).astype(BF16), keff, b, e_tail)
        yield
        lhs =each(lambda x, y: jnp.concatenate([_stack_heads(x, masks), _stack_heads(y, masks)],
                                                axis=0).astype(BF16), kkt, rt)
        rhs = each(lambda x, y: jnp.concatenate([_stack_heads(x, masks), _stack_heads(y, masks)],
                                                axis=0).astype(BF16), khat, bhat)
        amat = each(_dot_nt, lhs, rhs)
        yield
        akb =each(lambda m, d: jnp.where(strict[d], m[:SLAB, SLAB:], 0.0), amat, ds)
        a_v = each(lambda m, d: jnp.concatenate([jnp.where(strict[d], m[:SLAB, :SLAB], 0.0),
                                                 jnp.where(incl[d], m[SLAB:, :SLAB], 0.0)], axis=0).astype(BF16),
                   amat, ds)
        arb = each(lambda m, d: jnp.where(incl[d], m[SLAB:, SLAB:], 0.0).astype(BF16), amat, ds)
        av = each(lambda m, x: _dot(m, _stack_heads(x, masks).astype(BF16)), a_v, v)
        yield
        x = each(lambda m: eye - m, akb)
        p = each(lambda m: m.astype(BF16), akb)
        for _ in range(int(math.log2(c)) - 1):
            p32 = each(lambda m: _dot(m, m), p)
            p = each(lambda m: m.astype(BF16), p32)
            x = each(lambda xx, pp: xx + _dot(xx.astype(BF16), pp), x, p)
            yield
        xinv =each(lambda m: m.astype(BF16), x)
        yield
        mpack = each(lambda x, m: (_stack_heads(x, masks) + swap_halves(m[:SLAB])).astype(BF16), kkt, av)
        z_s = each(_dot, xinv, mpack)
        yield
        g_s = each(lambda ab, z: _dot(ab, z.astype(BF16)), arb, z_s)
        w1 = each(lambda z: pick(z, masks), z_s)
        w2 = each(lambda z: swap_halves(pick(z, far_masks)), z_s)
        yield
        rg = each(lambda x, g: x - pick(g, masks), rt, g_s)
        yconst = each(lambda m, g: _collapse_heads(m[SLAB:], c) - swap_halves(pick(g, far_masks)), av, g_s)
        lhs_b = each(lambda x, y: jnp.concatenate([x, y], axis=0).astype(BF16), rg, w1)
        return [dict(c0=c0s[i], d=ds[i], lhs=lhs_b[i], yconst=yconst[i], w2=w2[i], kb=kb[i], v=v[i],
                     decay=jnp.exp(tot[i]), bonus=bonus[i]) for i in range(len(lanes))]

    def stage_b(la, states):
        each = lambda f, *cols: [f(*args) for args in zip(*cols)]
        ks = each(lambda l, s: _dot_nt(l["lhs"](), s.astype(BF16)), la, states)
        yield
        y = each(lambda l, x: x[:c] + l["yconst"](), la, ks)
        vs_t = each(lambda l, x: jnp.concatenate([l["v"](), x[c:] + l["w2"]()], axis=0).T.astype(BF16), la, ks)
        yield
        upd = each(lambda l, x: jnp.where(same_head, _dot(x, l["kb"]()), 0.0), la, vs_t)
        yield
        return y, each(lambda l, s, x: s * l["decay"]() + x, la, states, upd)

    s_ref[...] = jnp.zeros((2, SLAB, SLAB), F32)
    u = RWKV_UNROLL
    n_groups = n_chunks // u

    def group_lanes(j):
        return [(j * u + q, 0) for q in range(u)] + [(n_chunks - 1 - j * u - q, 1) for q in range(u)]

    drain, run_together = _drain, _run_together

    def save(la):
        for i, l in enumerate(la):
            hb_ref[i, 0] = l["lhs"]
            hb_ref[i, 1] = l["kb"]
            for slot, key in enumerate(("yconst", "w2", "v", "bonus")):
                hf_ref[i, slot] = l[key]
            hd_ref[i] = l["decay"] + jnp.zeros((SUBLANES, SLAB), F32)

    def load(j):
        def lane(i, ci, d):
            get = lambda ref, *idx: (lambda: ref[idx])
            return dict(c0=pl.multiple_of(ci * c, c), d=d, lhs=get(hb_ref, i, 0), kb=get(hb_ref, i, 1),
                        yconst=get(hf_ref, i, 0), w2=get(hf_ref, i, 1), v=get(hf_ref, i, 2),
                        bonus=get(hf_ref, i, 3), decay=get(hd_ref, i, slice(0, 1)))
        return [lane(i, ci, d) for i, (ci, d) in enumerate(group_lanes(j))]

    def stage_b_group(la):
        out = []
        states = [s_ref[0], s_ref[1]]
        for q in range(u):
            pair = [la[q], la[u + q]]
            ys, states = yield from stage_b(pair, states)
            out += list(zip(pair, ys))
        s_ref[0], s_ref[1] = states
        return out

    def emit(res, finish):
        if not finish:
            for l, y in res:
                y_ref[pl.ds(l["c0"], c), :] = y
                bonus_ref[pl.ds(l["c0"], c), :] = l["bonus"]()
            return
        ys = [y + y_ref[pl.ds(l["c0"], c), :] for l, y in res]
        mean = [_seg_sum(y, bd_ones) * (1.0 / HEAD_DIM) for y in ys]
        yc = [y - m for y, m in zip(ys, mean)]
        var = [_seg_sum(z * z, bd_ones) * (1.0 / HEAD_DIM) for z in yc]
        for (l, _), z, vv in zip(res, yc, var):
            out = z * lax.rsqrt(vv + GN_EPS) * prow(0, _R_LG) + prow(0, _R_LB)
            o_ref[0, pl.ds(l["c0"], c), :] = (out + l["bonus"]() + bonus_ref[pl.ds(l["c0"], c), :]).astype(o_ref.dtype)

    def step(j, finish):
        res = drain(stage_b_group(load(j)))
        la_next = drain(stage_a(group_lanes(j + 1)))
        emit(res, finish)
        save(la_next)

    save(drain(stage_a(group_lanes(0))))
    lax.fori_loop(0, n_groups // 2, lambda j, carry: (step(j, False), carry)[1], 0)
    lax.fori_loop(n_groups // 2, n_groups - 1, lambda j, carry: (step(j, True), carry)[1], 0)
    emit(drain(stage_b_group(load(n_groups - 1))), True)


def _mixer_rwkv(p, mu, w0, w_up, a0, a_up, k_k, k_a, r_k, ln_g, ln_b):
    bsz, t, _ = p.shape
    assert t % (2 * CHUNK * RWKV_UNROLL) == 0
    n_lanes = 2 * RWKV_UNROLL
    slab = lambda a: a.reshape(a.shape[:-1] + (N_SLABS, SLAB))
    per_dir = [slab(mu[:, :W_TOK]), slab(mu[:, W_TOK:2 * W_TOK]), slab(mu[:, 2 * W_TOK:3 * W_TOK]),
               slab(w0), slab(a0)]
    shared = [slab(z.reshape(1, W_TOK)) for z in (k_k, k_a, r_k, ln_g, ln_b)]
    rows = per_dir + [jnp.broadcast_to(z, (2, N_SLABS, SLAB)) for z in shared]
    prm = jnp.stack(rows, axis=2)
    prm = jnp.pad(prm, ((0, 0), (0, 0), (0, _N_PROWS - prm.shape[2]), (0, 0)))
    prm = jnp.transpose(prm, (1, 0, 2, 3)).astype(F32)
    mu_wa = mu[:, 3 * W_TOK:].astype(F32)
    wu = jnp.transpose(slab(w_up), (2, 0, 1, 3))
    au = jnp.transpose(slab(a_up), (2, 0, 1, 3))
    wcat = jnp.concatenate([jnp.concatenate([wu, jnp.zeros_like(wu)], axis=-1),
                            jnp.concatenate([jnp.zeros_like(au), au], axis=-1)], axis=2).astype(BF16)
    n_wa = RANK_W + RANK_A
    return pl.pallas_call(
        functools.partial(_rwkv_kernel, t=t),
        grid=(bsz, N_SLABS),
        in_specs=[
            pl.BlockSpec((1, t, SLAB), lambda b, g: (b, 0, g)),
            pl.BlockSpec((1, t, SLAB), lambda b, g: (b, 0, N_SLABS + g)),
            pl.BlockSpec((1, t, SLAB), lambda b, g: (b, 0, 2 * N_SLABS + g)),
            pl.BlockSpec((1, t, n_wa), lambda b, g: (b, 0, 3 * W_TOK // n_wa)),
            pl.BlockSpec((1, 2, _N_PROWS, SLAB), lambda b, g: (g, 0, 0, 0)),
            pl.BlockSpec((2, n_wa), lambda b, g: (0, 0)),
            pl.BlockSpec((1, 2, n_wa, 2 * SLAB), lambda b, g: (g, 0, 0, 0)),
        ],
        out_specs=pl.BlockSpec((1, t, SLAB), lambda b, g: (b, 0, g)),
        out_shape=jax.ShapeDtypeStruct((bsz, t, W_TOK), BF16),
        scratch_shapes=[pltpu.VMEM((t, SLAB), F32), pltpu.VMEM((t, SLAB), F32), pltpu.VMEM((2, SLAB, SLAB), F32),
                        pltpu.VMEM((n_lanes, 2, 2 * CHUNK, SLAB), BF16), pltpu.VMEM((n_lanes, 4, CHUNK, SLAB), F32),
                        pltpu.VMEM((n_lanes, SUBLANES, SLAB), F32)],
        compiler_params=_params("arbitrary", "arbitrary"),
        name="mixer_rwkv",
    )(p, p, p, p, prm, mu_wa, wcat)


def _trunk(x, mem, pre_norm, post_norm, mem_norm, w_mem_kv,
           a_w_in, a_conv_w, a_conv_b, a_ln_g, a_ln_b, a_w_out,
           b_w_in, b_mu, b_w0, b_w_up, b_a0, b_a_up, b_k_k, b_k_a, b_r_k, b_ln_g, b_ln_b, b_w_out,
           c_w_in, c_rpb, c_w_out):
    kt, mv = _mem_kv(mem, mem_norm, w_mem_kv)
    t = x.shape[1]
    tm = min(TOKEN_TILE, t)
    for l in range(DEPTH):
        kind, j = l % 3, l // 3
        if kind == 0:
            x = _conv_layer(x, pre_norm[l], a_w_in[j], kt, mv, l, a_conv_w[j], a_conv_b[j], a_ln_g[j], a_ln_b[j],
                            a_w_out[j], post_norm[l], min(CONV_TILE, t))
        elif kind == 1:
            h, sg, mog = _in_proj(x, pre_norm[l], b_w_in[j], kt, mv, l, P_B, BF16, tm)
            tok = _mixer_rwkv(h, b_mu[j], b_w0[j], b_w_up[j], b_a0[j], b_a_up[j],
                              b_k_k[j], b_k_a[j], b_r_k[j], b_ln_g[j], b_ln_b[j])
            x = _out_proj(tok, sg, mog, x, b_w_out[j], post_norm[l], tm)
        else:
            h, sg, mog = _in_proj(x, pre_norm[l], c_w_in[j], kt, mv, l, 3 * W_TOK, BF16, tm)
            tok = _mixer_natten(h, c_rpb[j], min(WIN_R, t // GRID_W))
            x = _out_proj(tok, sg, mog, x, c_w_out[j], post_norm[l], tm)
    return x


def kernel(x_prompt, x_sample, mem_prompt, mem_sample, pre_norm, post_norm, mem_norm, w_mem_kv,
           a_w_in, a_conv_w, a_conv_b, a_ln_g, a_ln_b, a_w_out,
           b_w_in, b_mu, b_w0, b_w_up, b_a0, b_a_up, b_k_k, b_k_a, b_r_k, b_ln_g, b_ln_b, b_w_out,
           c_w_in, c_rpb, c_w_out):
    weights = (pre_norm, post_norm, mem_norm, w_mem_kv,
               a_w_in, a_conv_w, a_conv_b, a_ln_g, a_ln_b, a_w_out,
               b_w_in, b_mu, b_w0, b_w_up, b_a0, b_a_up, b_k_k, b_k_a, b_r_k, b_ln_g, b_ln_b, b_w_out,
               c_w_in, c_rpb, c_w_out)
    return (_trunk(x_prompt, mem_prompt, *weights), _trunk(x_sample, mem_sample, *weights))
```
